```python
import jax, jax.numpy as jnp
from jax import lax
import numpy as np

D_MODEL = 1024
BATCH = 8
SEQ = 4096
DEPTH = 2

N_MIXERS = 2
N_ATTN_LAYERS = (DEPTH + 1) // 2
N_POOL_LAYERS = DEPTH // 2
GRID_W = 64
EPS = 1e-6

HEAD_DIM = 64
N_HEADS = D_MODEL // HEAD_DIM
N_KV_HEADS = 4
GQA_GROUP = N_HEADS // N_KV_HEADS
Q_BLOCK = 128
ROPE_THETA = 10000.0
ROPE_PAIRS = HEAD_DIM // 4
QKV_DIM = (N_HEADS + 2 * N_KV_HEADS) * HEAD_DIM

POOL_WINDOWS = (2, 4, 8, 16)
N_POOL_GROUPS = len(POOL_WINDOWS)
POOL_GROUP_W = D_MODEL // N_POOL_GROUPS

MEM_LEN = 256
X_HEADS = 4
X_HEAD_DIM = D_MODEL // X_HEADS

D_FF = 2816
CONV_W = 3

kernel_name = "hybrid_axial_gqa_pool_encoder"


def rmsnorm(x, gain):
    x32 = x.astype(jnp.float32)
    y = x32 * lax.rsqrt(jnp.mean(x32 * x32, axis=-1, keepdims=True) + EPS)
    return (y * gain.astype(jnp.float32)).astype(x.dtype)


def axial_rope_tables(seq_len):
    n_rows = seq_len // GRID_W
    row = jnp.repeat(jnp.arange(n_rows, dtype=jnp.float32), GRID_W)
    col = jnp.tile(jnp.arange(GRID_W, dtype=jnp.float32), n_rows)
    inv_freq = ROPE_THETA ** (-jnp.arange(ROPE_PAIRS, dtype=jnp.float32) / ROPE_PAIRS)
    ang = jnp.stack([row[:, None] * inv_freq, col[:, None] * inv_freq], axis=1)
    return jnp.cos(ang), jnp.sin(ang)


def apply_axial_rope(x, cos, sin):
    b, s, h, d = x.shape
    xs = x.astype(jnp.float32).reshape(b, s, h, 2, 2, ROPE_PAIRS)
    x1, x2 = xs[..., 0, :], xs[..., 1, :]
    c = cos[None, :, None]
    sn = sin[None, :, None]
    out = jnp.stack([x1 * c - x2 * sn, x2 * c + x1 * sn], axis=-2)
    return out.reshape(b, s, h, d).astype(x.dtype)


def axial_gqa_attention(h, w_qkv, q_gain, k_gain, w_o):
    b, s, _ = h.shape
    qkv = h @ w_qkv
    q_end = N_HEADS * HEAD_DIM
    k_end = q_end + N_KV_HEADS * HEAD_DIM
    q = qkv[..., :q_end].reshape(b, s, N_HEADS, HEAD_DIM)
    k = qkv[..., q_end:k_end].reshape(b, s, N_KV_HEADS, HEAD_DIM)
    v = qkv[..., k_end:].reshape(b, s, N_KV_HEADS, HEAD_DIM)
    q = rmsnorm(q, q_gain)
    k = rmsnorm(k, k_gain)
    cos, sin = axial_rope_tables(s)
    q = apply_axial_rope(q, cos, sin) * (HEAD_DIM ** -0.5)
    k = apply_axial_rope(k, cos, sin)
    n_blk = s // Q_BLOCK
    qb = q.reshape(b, n_blk, Q_BLOCK, N_KV_HEADS, GQA_GROUP, HEAD_DIM).transpose(1, 0, 3, 4, 2, 5)
    kt = k.transpose(0, 2, 1, 3)
    vt = v.transpose(0, 2, 1, 3)

    def attend_block(q_blk):
        scores = jnp.einsum('bkgqd,bksd->bkgqs', q_blk, kt).astype(jnp.float32)
        p = jax.nn.softmax(scores, axis=-1).astype(vt.dtype)
        return jnp.einsum('bkgqs,bksd->bkgqd', p, vt)

    o = lax.map(attend_block, qb)
    o = o.transpose(1, 0, 4, 2, 3, 5).reshape(b, s, N_HEADS * HEAD_DIM)
    return o @ w_o


def multiscale_pool_mixer(h, w_grp, scale):
    b, s, d = h.shape
    hg = h.astype(jnp.float32).reshape(b, s, N_POOL_GROUPS, POOL_GROUP_W)
    csum = jnp.concatenate(
        [jnp.zeros((b, 1, N_POOL_GROUPS, POOL_GROUP_W), jnp.float32), jnp.cumsum(hg, axis=1)], axis=1)
    t = jnp.arange(s)[:, None]
    win = jnp.array(POOL_WINDOWS, dtype=jnp.int32)[None, :]
    lo = jnp.clip(t - win // 2, 0, s)
    hi = jnp.clip(t + win - win // 2, 0, s)
    s_hi = jnp.take_along_axis(csum, hi[None, :, :, None], axis=1)
    s_lo = jnp.take_along_axis(csum, lo[None, :, :, None], axis=1)
    count = (hi - lo).astype(jnp.float32)[None, :, :, None]
    mixed = (s_hi - s_lo) / count - hg
    y = jnp.einsum('bsgc,gcd->bsgd', mixed, w_grp.astype(jnp.float32)).reshape(b, s, d)
    return (y * scale.astype(jnp.float32)).astype(h.dtype)


def memory_cross_attention(h, mem_n, w_q, w_kv, w_o):
    b, s, _ = h.shape
    m = mem_n.shape[1]
    q = (h @ w_q).reshape(b, s, X_HEADS, X_HEAD_DIM) * (X_HEAD_DIM ** -0.5)
    kv = mem_n @ w_kv
    k = kv[..., :D_MODEL].reshape(b, m, X_HEADS, X_HEAD_DIM)
    v = kv[..., D_MODEL:].reshape(b, m, X_HEADS, X_HEAD_DIM)
    scores = jnp.einsum('bshd,bmhd->bhsm', q, k).astype(jnp.float32)
    p = jax.nn.softmax(scores, axis=-1).astype(v.dtype)
    o = jnp.einsum('bhsm,bmhd->bshd', p, v).reshape(b, s, D_MODEL)
    return o @ w_o


def conv_gated_ffn(h, w_up, conv_w, conv_b, w_down):
    u = h @ w_up
    up = jnp.pad(u, ((0, 0), (1, 1), (0, 0)))
    u = up[:, :-2] * conv_w[0] + up[:, 1:-1] * conv_w[1] + up[:, 2:] * conv_w[2] + conv_b
    gate, val = u[..., :D_FF], u[..., D_FF:]
    return (jax.nn.silu(gate) * val) @ w_down


def setup_inputs(seed: int = 0) -> dict:
    key = jax.random.key(seed)
    ks = jax.random.split(key, 24)
    f32 = jnp.float32

    def nrm(k, shape, scale):
        return jax.random.normal(k, shape, f32) * scale

    def gain(k, shape):
        return 1.0 + 0.05 * jax.random.normal(k, shape, f32)

    na, nb = N_ATTN_LAYERS, N_POOL_LAYERS
    return {
        "x": nrm(ks[0], (BATCH, SEQ, D_MODEL), 1.0),
        "mem": nrm(ks[1], (BATCH, MEM_LEN, D_MODEL), 1.0),
        "attn_norm": gain(ks[2], (na, D_MODEL)),
        "attn_w_qkv": nrm(ks[3], (na, D_MODEL, QKV_DIM), D_MODEL ** -0.5),
        "attn_q_gain": gain(ks[4], (na, HEAD_DIM)),
        "attn_k_gain": gain(ks[5], (na, HEAD_DIM)),
        "attn_w_o": nrm(ks[6], (na, N_HEADS * HEAD_DIM, D_MODEL), (N_HEADS * HEAD_DIM) ** -0.5),
        "pool_norm": gain(ks[7], (nb, D_MODEL)),
        "pool_w": nrm(ks[8], (nb, N_POOL_GROUPS, POOL_GROUP_W, POOL_GROUP_W), POOL_GROUP_W ** -0.5),
        "pool_scale": gain(ks[9], (nb, D_MODEL)),
        "xattn_norm": gain(ks[10], (DEPTH, D_MODEL)),
        "mem_norm": gain(ks[11], (DEPTH, D_MODEL)),
        "xattn_w_q": nrm(ks[12], (DEPTH, D_MODEL, D_MODEL), D_MODEL ** -0.5),
        "xattn_w_kv": nrm(ks[13], (DEPTH, D_MODEL, 2 * D_MODEL), D_MODEL ** -0.5),
        "xattn_w_o": nrm(ks[14], (DEPTH, D_MODEL, D_MODEL), D_MODEL ** -0.5),
        "ffn_norm": gain(ks[15], (DEPTH, D_MODEL)),
        "ffn_w_up": nrm(ks[16], (DEPTH, D_MODEL, 2 * D_FF), D_MODEL ** -0.5),
        "ffn_conv_w": nrm(ks[17], (DEPTH, CONV_W, 2 * D_FF), CONV_W ** -0.5),
        "ffn_conv_b": nrm(ks[18], (DEPTH, 2 * D_FF), 0.02),
        "ffn_w_down": nrm(ks[19], (DEPTH, D_FF, D_MODEL), D_FF ** -0.5),
        "final_norm": gain(ks[20], (D_MODEL,)),
    }


def reference(x, mem, attn_norm, attn_w_qkv, attn_q_gain, attn_k_gain, attn_w_o,
              pool_norm, pool_w, pool_scale,
              xattn_norm, mem_norm, xattn_w_q, xattn_w_kv, xattn_w_o,
              ffn_norm, ffn_w_up, ffn_conv_w, ffn_conv_b, ffn_w_down,
              final_norm):
    ia = 0
    ib = 0
    for i in range(DEPTH):
        if i % N_MIXERS == 0:
            x = x + axial_gqa_attention(rmsnorm(x, attn_norm[ia]), attn_w_qkv[ia],
                                        attn_q_gain[ia], attn_k_gain[ia], attn_w_o[ia])
            ia += 1
        else:
            x = x + multiscale_pool_mixer(rmsnorm(x, pool_norm[ib]), pool_w[ib], pool_scale[ib])
            ib += 1
        x = x + memory_cross_attention(rmsnorm(x, xattn_norm[i]), rmsnorm(mem, mem_norm[i]),
                                       xattn_w_q[i], xattn_w_kv[i], xattn_w_o[i])
        x = x + conv_gated_ffn(rmsnorm(x, ffn_norm[i]), ffn_w_up[i], ffn_conv_w[i],
                               ffn_conv_b[i], ffn_w_down[i])
    return rmsnorm(x, final_norm)
```

```python
import functools
import math

import jax
import jax.numpy as jnp
from jax import lax
from jax.experimental import pallas as pl
from jax.experimental.pallas import tpu as pltpu

F32 = jnp.float32
BF16 = jnp.bfloat16

D_MODEL = 1024
GRID_W = 64
EPS = 1e-6
HEAD_DIM = 64
N_HEADS = 16
N_KV_HEADS = 4
GQA_GROUP = 4
ROPE_THETA = 10000.0
ROPE_PAIRS = 16
Q_DIM = N_HEADS * HEAD_DIM
KV_DIM = N_KV_HEADS * HEAD_DIM
QK_DIM = Q_DIM + KV_DIM
QKV_DIM = Q_DIM + 2 * KV_DIM
POOL_WINDOWS = (2, 4, 8, 16)
POOL_GROUP_W = 256
MEM_LEN = 256
X_HEADS = 4
X_HEAD_DIM = 256
D_FF = 2816
LOG2E = math.log2(math.e)

LANES = 128
HALO = 16
VMEM_LIMIT = 56 * 1024 * 1024

TM = 512
TQ = 256
TK = 256
FF_CHUNK = 256


def _rms(x, gain):
    return x * lax.rsqrt(jnp.mean(x * x, axis=-1, keepdims=True) + EPS) * gain


def _params(*sem):
    return pltpu.CompilerParams(dimension_semantics=sem, vmem_limit_bytes=VMEM_LIMIT)


def _const_spec(shape):
    nd = len(shape)
    return pl.BlockSpec(shape, lambda *_: (0,) * nd)


def _qkv_kernel(x_ref, g_ref, w_ref, qg_ref, kg_ref, cos_ref, sin_ref, e1_ref, e2_ref,
                qT_ref, k_ref, vT_ref):
    h = _rms(x_ref[...], g_ref[...])
    y = jnp.dot(h.astype(BF16), w_ref[...], preferred_element_type=F32)
    qk = y[:, :QK_DIM]
    ms = jnp.dot((qk * qk).astype(BF16), e1_ref[...], preferred_element_type=F32)
    r = lax.rsqrt(ms + EPS)
    head = lax.broadcasted_iota(jnp.int32, (1, LANES), 1)
    r = r * jnp.where(head < N_HEADS, (HEAD_DIM ** -0.5) * LOG2E, 1.0)
    r_hi = r.astype(BF16)
    r_lo = (r - r_hi.astype(F32)).astype(BF16)
    rb = jnp.dot(jnp.concatenate([r_hi, r_lo], axis=1), e2_ref[...],
                 preferred_element_type=F32)
    cos = cos_ref[...]
    sin = sin_ref[...]
    lane = lax.broadcasted_iota(jnp.int32, cos.shape, 1)
    first = (lane % 32) < ROPE_PAIRS
    outs = []
    for j in range(QK_DIM // LANES):
        sl = slice(j * LANES, (j + 1) * LANES)
        yb = qk[:, sl] * (qg_ref[...] if j < Q_DIM // LANES else kg_ref[...])
        partner = jnp.where(first, pltpu.roll(yb, LANES - ROPE_PAIRS, 1),
                            pltpu.roll(yb, ROPE_PAIRS, 1))
        outs.append((yb * cos + partner * sin) * rb[:, sl])
    q = jnp.concatenate(outs[:Q_DIM // LANES], axis=1)
    k = jnp.concatenate(outs[Q_DIM // LANES:], axis=1)
    qT_ref[0] = q.T.astype(BF16)
    vT_ref[0] = y[:, QK_DIM:].T.astype(BF16)
    for g in range(N_KV_HEADS):
        k_ref[0, g] = k[:, g * HEAD_DIM:(g + 1) * HEAD_DIM].astype(BF16)


def _qkv_call(x, gain, w, qg, kg, cos, sin, e1, e2, batch, seq):
    nt = seq // TM
    return pl.pallas_call(
        _qkv_kernel,
        grid=(batch * nt,),
        in_specs=[
            pl.BlockSpec((TM, D_MODEL), lambda i: (i, 0)),
            _const_spec((1, D_MODEL)),
            _const_spec((D_MODEL, QKV_DIM)),
            _const_spec((1, LANES)),
            _const_spec((1, LANES)),
            pl.BlockSpec((TM, LANES), lambda i: (i % nt, 0)),
            pl.BlockSpec((TM, LANES), lambda i: (i % nt, 0)),
            _const_spec((QK_DIM, LANES)),
            _const_spec((2 * LANES, QK_DIM)),
        ],
        out_specs=[
            pl.BlockSpec((1, Q_DIM, TM), lambda i: (i // nt, 0, i % nt)),
            pl.BlockSpec((1, N_KV_HEADS, TM, HEAD_DIM), lambda i: (i // nt, 0, i % nt, 0)),
            pl.BlockSpec((1, KV_DIM, TM), lambda i: (i // nt, 0, i % nt)),
        ],
        out_shape=[
            jax.ShapeDtypeStruct((batch, Q_DIM, seq), BF16),
            jax.ShapeDtypeStruct((batch, N_KV_HEADS, seq, HEAD_DIM), BF16),
            jax.ShapeDtypeStruct((batch, KV_DIM, seq), BF16),
        ],
        compiler_params=_params("parallel"),
        name="qkv_rope",
    )(x, gain, w, qg, kg, cos, sin, e1, e2)


def _attn_kernel(qT_ref, k_ref, vT_ref, oT_ref, m_ref, l_ref, acc_ref, *, seq):
    q = jnp.concatenate(
        [qT_ref[0, g * HEAD_DIM:(g + 1) * HEAD_DIM, :] for g in range(GQA_GROUP)], axis=1)
    m_ref[...] = jnp.full(m_ref.shape, -1e30, F32)
    l_ref[...] = jnp.zeros(l_ref.shape, F32)
    acc_ref[...] = jnp.zeros(acc_ref.shape, F32)

    def body(c, carry):
        off = pl.multiple_of(c * TK, TK)
        kc = k_ref[0, 0, pl.ds(off, TK), :]
        s = jnp.dot(kc, q, preferred_element_type=F32)
        m_prev = m_ref[...]
        m_new = jnp.maximum(m_prev, jnp.max(s, axis=0, keepdims=True))
        alpha = jnp.exp2(m_prev - m_new)
        p = jnp.exp2(s - m_new)
        l_ref[...] = alpha * l_ref[...] + jnp.sum(p, axis=0, keepdims=True)
        vc = vT_ref[0, :, pl.ds(off, TK)]
        acc_ref[...] = alpha * acc_ref[...] + jnp.dot(
            vc, p.astype(BF16), preferred_element_type=F32)
        m_ref[...] = m_new
        return carry

    lax.fori_loop(0, seq // TK, body, 0)
    o = acc_ref[...] / l_ref[...]
    for g in range(GQA_GROUP):
        oT_ref[0, g * HEAD_DIM:(g + 1) * HEAD_DIM, :] = o[:, g * TQ:(g + 1) * TQ].astype(BF16)


def _attn_call(qT, k, vT, batch, seq):
    grp = GQA_GROUP * HEAD_DIM
    return pl.pallas_call(
        functools.partial(_attn_kernel, seq=seq),
        grid=(batch, N_KV_HEADS, seq // TQ),
        in_specs=[
            pl.BlockSpec((1, grp, TQ), lambda b, h, i: (b, h, i)),
            pl.BlockSpec((1, 1, seq, HEAD_DIM), lambda b, h, i: (b, h, 0, 0)),
            pl.BlockSpec((1, HEAD_DIM, seq), lambda b, h, i: (b, h, 0)),
        ],
        out_specs=pl.BlockSpec((1, grp, TQ), lambda b, h, i: (b, h, i)),
        out_shape=jax.ShapeDtypeStruct((batch, Q_DIM, seq), BF16),
        scratch_shapes=[
            pltpu.VMEM((1, GQA_GROUP * TQ), F32),
            pltpu.VMEM((1, GQA_GROUP * TQ), F32),
            pltpu.VMEM((HEAD_DIM, GQA_GROUP * TQ), F32),
        ],
        compiler_params=_params("parallel", "parallel", "parallel"),
        name="axial_attention",
    )(qT, k, vT)


def _memkv_kernel(mem_ref, g_ref, w_ref, kT_ref, v_ref):
    h = _rms(mem_ref[0], g_ref[0])
    kv = jnp.dot(h.astype(BF16), w_ref[0], preferred_element_type=F32)
    kT_ref[0, 0] = kv[:, :D_MODEL].T.astype(BF16)
    v_ref[0, 0] = kv[:, D_MODEL:].astype(BF16)


def _memkv_call(mem, gain, w, depth, batch):
    return pl.pallas_call(
        _memkv_kernel,
        grid=(depth, batch),
        in_specs=[
            pl.BlockSpec((1, MEM_LEN, D_MODEL), lambda l, b: (b, 0, 0)),
            pl.BlockSpec((1, 1, D_MODEL), lambda l, b: (l, 0, 0)),
            pl.BlockSpec((1, D_MODEL, 2 * D_MODEL), lambda l, b: (l, 0, 0)),
        ],
        out_specs=[
            pl.BlockSpec((1, 1, D_MODEL, MEM_LEN), lambda l, b: (l, b, 0, 0)),
            pl.BlockSpec((1, 1, MEM_LEN, D_MODEL), lambda l, b: (l, b, 0, 0)),
        ],
        out_shape=[
            jax.ShapeDtypeStruct((depth, batch, D_MODEL, MEM_LEN), BF16),
            jax.ShapeDtypeStruct((depth, batch, MEM_LEN, D_MODEL), BF16),
        ],
        compiler_params=_params("parallel", "parallel"),
        name="memory_kv",
    )(mem, gain, w)


def _xattn(x1, gx_ref, wq_ref, kT_ref, v_ref, wo_ref):
    h = _rms(x1, gx_ref[...])
    q = jnp.dot(h.astype(BF16), wq_ref[...], preferred_element_type=F32)
    q = q * ((X_HEAD_DIM ** -0.5) * LOG2E)
    outs = []
    for hd in range(X_HEADS):
        sl = slice(hd * X_HEAD_DIM, (hd + 1) * X_HEAD_DIM)
        s = jnp.dot(q[:, sl].astype(BF16), kT_ref[0, 0, sl, :],
                    preferred_element_type=F32)
        p = jnp.exp2(s - jnp.max(s, axis=-1, keepdims=True))
        l = jnp.sum(p, axis=-1, keepdims=True)
        o = jnp.dot(p.astype(BF16), v_ref[0, 0, :, sl], preferred_element_type=F32)
        outs.append((o / l).astype(BF16))
    o = jnp.concatenate(outs, axis=1)
    return x1 + jnp.dot(o, wo_ref[...], preferred_element_type=F32)


def _attn_out_xattn_kernel(x_ref, oT_ref, wo_ref, gx_ref, wq_ref, kT_ref, v_ref, wo2_ref,
                           out_ref):
    o = oT_ref[0].T
    x1 = x_ref[...] + jnp.dot(o, wo_ref[...], preferred_element_type=F32)
    out_ref[...] = _xattn(x1, gx_ref, wq_ref, kT_ref, v_ref, wo2_ref)


def _xattn_specs(layer, nt):
    return [
        _const_spec((1, D_MODEL)),
        _const_spec((D_MODEL, D_MODEL)),
        pl.BlockSpec((1, 1, D_MODEL, MEM_LEN), lambda i: (layer, i // nt, 0, 0)),
        pl.BlockSpec((1, 1, MEM_LEN, D_MODEL), lambda i: (layer, i // nt, 0, 0)),
        _const_spec((D_MODEL, D_MODEL)),
    ]


def _attn_out_xattn_call(x, oT, wo, gx, wq, kT, v, wo2, layer, batch, seq):
    nt = seq // TM
    return pl.pallas_call(
        _attn_out_xattn_kernel,
        grid=(batch * nt,),
        in_specs=[
            pl.BlockSpec((TM, D_MODEL), lambda i: (i, 0)),
            pl.BlockSpec((1, Q_DIM, TM), lambda i: (i // nt, 0, i % nt)),
            _const_spec((Q_DIM, D_MODEL)),
        ] + _xattn_specs(layer, nt),
        out_specs=pl.BlockSpec((TM, D_MODEL), lambda i: (i, 0)),
        out_shape=jax.ShapeDtypeStruct(x.shape, F32),
        compiler_params=_params("parallel"),
        name="attn_out_xattn",
    )(x, oT, wo, gx, wq, kT, v, wo2)


def _normed_with_halo(xp_ref, x_ref, xn_ref, gain, nt):
    i = pl.program_id(0)
    keep_prev = (i % nt != 0).astype(F32)
    keep_next = (i % nt != nt - 1).astype(F32)
    hp = _rms(xp_ref[...], gain) * keep_prev
    hn = _rms(xn_ref[...], gain) * keep_next
    return jnp.concatenate([hp, _rms(x_ref[...], gain), hn], axis=0)


def _halo_in_specs(n_tiles):
    blocks = TM // HALO
    n_halo_blocks = n_tiles * blocks
    return [
        pl.BlockSpec((HALO, D_MODEL), lambda i: (jnp.maximum(i * blocks - 1, 0), 0)),
        pl.BlockSpec((TM, D_MODEL), lambda i: (i, 0)),
        pl.BlockSpec((HALO, D_MODEL),
                     lambda i: (jnp.minimum((i + 1) * blocks, n_halo_blocks - 1), 0)),
    ]


def _pool_xattn_kernel(xp_ref, x_ref, xn_ref, gp_ref, wp_ref, sc_ref,
                       gx_ref, wq_ref, kT_ref, v_ref, wo2_ref, out_ref, *, nt, seq):
    i = pl.program_id(0)
    rows = TM + 2 * HALO
    h = _normed_with_halo(xp_ref, x_ref, xn_ref, gp_ref[...], nt)
    pos = (i % nt) * TM + lax.broadcasted_iota(jnp.int32, (TM, 1), 0)
    ys = []
    for g, w in enumerate(POOL_WINDOWS):
        sl = slice(g * POOL_GROUP_W, (g + 1) * POOL_GROUP_W)
        hg = h[:, sl]
        acc = hg
        span = 1
        while span < w:
            acc = acc + pltpu.roll(acc, span, 0)
            span *= 2
        shift = w // 2 - 1
        if shift:
            acc = pltpu.roll(acc, rows - shift, 0)
        cnt = (jnp.minimum(pos + w // 2, seq) - jnp.maximum(pos - w // 2, 0)).astype(F32)
        mixed = acc[HALO:HALO + TM] / cnt - hg[HALO:HALO + TM]
        ys.append(jnp.dot(mixed.astype(BF16), wp_ref[g], preferred_element_type=F32))
    y = jnp.concatenate(ys, axis=1) * sc_ref[...]
    x1 = x_ref[...] + y
    out_ref[...] = _xattn(x1, gx_ref, wq_ref, kT_ref, v_ref, wo2_ref)


def _pool_xattn_call(x, gp, wp, sc, gx, wq, kT, v, wo2, layer, batch, seq):
    nt = seq // TM
    return pl.pallas_call(
        functools.partial(_pool_xattn_kernel, nt=nt, seq=seq),
        grid=(batch * nt,),
        in_specs=_halo_in_specs(batch * nt) + [
            _const_spec((1, D_MODEL)),
            _const_spec((len(POOL_WINDOWS), POOL_GROUP_W, POOL_GROUP_W)),
            _const_spec((1, D_MODEL)),
        ] + _xattn_specs(layer, nt),
        out_specs=pl.BlockSpec((TM, D_MODEL), lambda i: (i, 0)),
        out_shape=jax.ShapeDtypeStruct(x.shape, F32),
        compiler_params=_params("parallel"),
        name="pool_xattn",
    )(x, x, x, gp, wp, sc, gx, wq, kT, v, wo2)


def _ffn_kernel(xp_ref, x_ref, xn_ref, g_ref, wup_ref, cw_ref, cb_ref, wdn_ref, gf_ref,
                out_ref, h_ref, act_ref, *, nt, final_norm):
    rows = TM + 2 * HALO
    h_ref[...] = _normed_with_halo(xp_ref, x_ref, xn_ref, g_ref[...], nt).astype(BF16)

    def conv(u, off):
        cw = cw_ref[:, pl.ds(off, FF_CHUNK)]
        c = (pltpu.roll(u, 1, 0) * cw[0:1] + u * cw[1:2]
             + pltpu.roll(u, rows - 1, 0) * cw[2:3] + cb_ref[:, pl.ds(off, FF_CHUNK)])
        return c[HALO:HALO + TM]

    def body(c, carry):
        goff = pl.multiple_of(c * FF_CHUNK, FF_CHUNK)
        voff = pl.multiple_of(D_FF + c * FF_CHUNK, FF_CHUNK)
        hb = h_ref[...]
        ug = jnp.dot(hb, wup_ref[:, pl.ds(goff, FF_CHUNK)], preferred_element_type=F32)
        uv = jnp.dot(hb, wup_ref[:, pl.ds(voff, FF_CHUNK)], preferred_element_type=F32)
        gate = conv(ug, goff)
        val = conv(uv, voff)
        act = gate * (1.0 / (1.0 + jnp.exp(-gate))) * val
        act_ref[:, pl.ds(goff, FF_CHUNK)] = act.astype(BF16)
        return carry

    lax.fori_loop(0, D_FF // FF_CHUNK, body, 0)
    y = x_ref[...] + jnp.dot(act_ref[...], wdn_ref[...], preferred_element_type=F32)
    if final_norm:
        y = _rms(y, gf_ref[...])
    out_ref[...] = y


def _ffn_call(x, gain, wup, cw, cb, wdn, gf, final_norm, batch, seq):
    nt = seq // TM
    return pl.pallas_call(
        functools.partial(_ffn_kernel, nt=nt, final_norm=final_norm),
        grid=(batch * nt,),
        in_specs=_halo_in_specs(batch * nt) + [
            _const_spec((1, D_MODEL)),
            _const_spec((D_MODEL, 2 * D_FF)),
            _const_spec((3, 2 * D_FF)),
            _const_spec((1, 2 * D_FF)),
            _const_spec((D_FF, D_MODEL)),
            _const_spec((1, D_MODEL)),
        ],
        out_specs=pl.BlockSpec((TM, D_MODEL), lambda i: (i, 0)),
        out_shape=jax.ShapeDtypeStruct(x.shape, F32),
        scratch_shapes=[
            pltpu.VMEM((TM + 2 * HALO, D_MODEL), BF16),
            pltpu.VMEM((TM, D_FF), BF16),
        ],
        compiler_params=_params("parallel"),
        name="conv_ffn",
    )(x, x, x, gain, wup, cw, cb, wdn, gf)


def _rope_tables(seq):
    t = jnp.arange(seq, dtype=jnp.int32)
    row = (t // GRID_W).astype(F32)
    col = (t % GRID_W).astype(F32)
    inv_freq = ROPE_THETA ** (-jnp.arange(ROPE_PAIRS, dtype=F32) / ROPE_PAIRS)
    ang_row = row[:, None] * inv_freq
    ang_col = col[:, None] * inv_freq
    ang = jnp.concatenate([ang_row, ang_row, ang_col, ang_col], axis=1)
    sign = jnp.concatenate([-jnp.ones((ROPE_PAIRS,), F32), jnp.ones((ROPE_PAIRS,), F32)] * 2)
    cos = jnp.tile(jnp.cos(ang), (1, 2))
    sin = jnp.tile(jnp.sin(ang) * sign, (1, 2))
    return cos, sin


def _head_indicators():
    col_head = jnp.arange(QK_DIM) // HEAD_DIM
    e = (col_head[:, None] == jnp.arange(LANES)[None, :]).astype(F32)
    e1 = (e / HEAD_DIM).astype(BF16)
    e2 = jnp.concatenate([e.T, e.T], axis=0).astype(BF16)
    return e1, e2


def kernel(x, mem, attn_norm, attn_w_qkv, attn_q_gain, attn_k_gain, attn_w_o, pool_norm, pool_w, pool_scale, xattn_norm, mem_norm, xattn_w_q, xattn_w_kv, xattn_w_o, ffn_norm, ffn_w_up, ffn_conv_w, ffn_conv_b, ffn_w_down, final_norm):
    batch, seq, d = x.shape
    depth = xattn_norm.shape[0]
    assert d == D_MODEL and seq % TM == 0 and seq % TQ == 0 and seq % TK == 0
    xf = x.reshape(batch * seq, d)

    cos, sin = _rope_tables(seq)
    e1, e2 = _head_indicators()
    kT_mem, v_mem = _memkv_call(mem, mem_norm.reshape(depth, 1, d),
                                xattn_w_kv.astype(BF16), depth, batch)

    ia = ib = 0
    for i in range(depth):
        xa = (xattn_norm[i].reshape(1, d), xattn_w_q[i].astype(BF16), kT_mem, v_mem,
              xattn_w_o[i].astype(BF16))
        if i % 2 == 0:
            qT, k, vT = _qkv_call(
                xf, attn_norm[ia].reshape(1, d), attn_w_qkv[ia].astype(BF16),
                jnp.tile(attn_q_gain[ia], 2).reshape(1, LANES),
                jnp.tile(attn_k_gain[ia], 2).reshape(1, LANES),
                cos, sin, e1, e2, batch, seq)
            oT = _attn_call(qT, k, vT, batch, seq)
            xf = _attn_out_xattn_call(xf, oT, attn_w_o[ia].astype(BF16), *xa, i, batch, seq)
            ia += 1
        else:
            xf = _pool_xattn_call(xf, pool_norm[ib].reshape(1, d), pool_w[ib].astype(BF16),
                                  pool_scale[ib].reshape(1, d), *xa, i, batch, seq)
            ib += 1
        xf = _ffn_call(xf, ffn_norm[i].reshape(1, d), ffn_w_up[i].astype(BF16),
                       ffn_conv_w[i], ffn_conv_b[i].reshape(1, 2 * D_FF),
                       ffn_w_down[i].astype(BF16), final_norm.reshape(1, d),
                       i == depth - 1, batch, seq)
    return xf.reshape(batch, seq, d)
```

```python
import functools
import math

import jax
import jax.numpy as jnp
from jax import lax
from jax.experimental import pallas as pl
from jax.experimental.pallas import tpu as pltpu

F32 = jnp.float32
BF16 = jnp.bfloat16

D_MODEL = 1024
GRID_W = 64
EPS = 1e-6
HEAD_DIM = 64
N_HEADS = 16
N_KV_HEADS = 4
GQA_GROUP = 4
ROPE_THETA = 10000.0
ROPE_PAIRS = 16
Q_DIM = N_HEADS * HEAD_DIM
KV_DIM = N_KV_HEADS * HEAD_DIM
QK_DIM = Q_DIM + KV_DIM
QKV_DIM = Q_DIM + 2 * KV_DIM
POOL_WINDOWS = (2, 4, 8, 16)
POOL_GROUP_W = 256
MEM_LEN = 256
X_HEADS = 4
X_HEAD_DIM = 256
D_FF = 2816
LOG2E = math.log2(math.e)

LANES = 128
HALO = 16
SUM_ROWS = 16
VMEM_LIMIT = 56 * 1024 * 1024

TM = 512
TQ = 256
TK = 256
FF_CHUNK = 256


def _rms(x, gain):
    return x * lax.rsqrt(jnp.mean(x * x, axis=-1, keepdims=True) + EPS) * gain


def _params(*sem):
    return pltpu.CompilerParams(dimension_semantics=sem, vmem_limit_bytes=VMEM_LIMIT)


def _const_spec(shape):
    nd = len(shape)
    return pl.BlockSpec(shape, lambda *_: (0,) * nd)


def _qkv_kernel(x_ref, g_ref, w_ref, qg_ref, kg_ref, cos_ref, sin_ref, e1_ref, e2_ref,
                qT_ref, k_ref, vT_ref):
    h = _rms(x_ref[...], g_ref[...])
    y = jnp.dot(h.astype(BF16), w_ref[...], preferred_element_type=F32)
    qk = y[:, :QK_DIM]
    ms = jnp.dot((qk * qk).astype(BF16), e1_ref[...], preferred_element_type=F32)
    r = lax.rsqrt(ms + EPS)
    head = lax.broadcasted_iota(jnp.int32, (1, LANES), 1)
    r = r * jnp.where(head < N_HEADS, (HEAD_DIM ** -0.5) * LOG2E, 1.0)
    r_hi = r.astype(BF16)
    r_lo = (r - r_hi.astype(F32)).astype(BF16)
    rb = jnp.dot(jnp.concatenate([r_hi, r_lo], axis=1), e2_ref[...],
                 preferred_element_type=F32)
    cos = cos_ref[...]
    sin = sin_ref[...]
    lane = lax.broadcasted_iota(jnp.int32, cos.shape, 1)
    first = (lane % 32) < ROPE_PAIRS
    outs = []
    for j in range(QK_DIM // LANES):
        sl = slice(j * LANES, (j + 1) * LANES)
        yb = qk[:, sl] * (qg_ref[...] if j < Q_DIM // LANES else kg_ref[...])
        partner = jnp.where(first, pltpu.roll(yb, LANES - ROPE_PAIRS, 1),
                            pltpu.roll(yb, ROPE_PAIRS, 1))
        outs.append((yb * cos + partner * sin) * rb[:, sl])
    q = jnp.concatenate(outs[:Q_DIM // LANES], axis=1)
    k = jnp.concatenate(outs[Q_DIM // LANES:], axis=1)
    qT_ref[0] = q.T.astype(BF16)
    vT_ref[0] = y[:, QK_DIM:].T.astype(BF16)
    for g in range(N_KV_HEADS):
        k_ref[0, g] = k[:, g * HEAD_DIM:(g + 1) * HEAD_DIM].astype(BF16)


def _qkv_call(x, gain, w, qg, kg, cos, sin, e1, e2, batch, seq):
    nt = seq // TM
    return pl.pallas_call(
        _qkv_kernel,
        grid=(batch * nt,),
        in_specs=[
            pl.BlockSpec((TM, D_MODEL), lambda i: (i, 0)),
            _const_spec((1, D_MODEL)),
            _const_spec((D_MODEL, QKV_DIM)),
            _const_spec((1, LANES)),
            _const_spec((1, LANES)),
            pl.BlockSpec((TM, LANES), lambda i: (i % nt, 0)),
            pl.BlockSpec((TM, LANES), lambda i: (i % nt, 0)),
            _const_spec((QK_DIM, LANES)),
            _const_spec((2 * LANES, QK_DIM)),
        ],
        out_specs=[
            pl.BlockSpec((1, Q_DIM, TM), lambda i: (i // nt, 0, i % nt)),
            pl.BlockSpec((1, N_KV_HEADS, TM, HEAD_DIM), lambda i: (i // nt, 0, i % nt, 0)),
            pl.BlockSpec((1, KV_DIM, TM), lambda i: (i // nt, 0, i % nt)),
        ],
        out_shape=[
            jax.ShapeDtypeStruct((batch, Q_DIM, seq), BF16),
            jax.ShapeDtypeStruct((batch, N_KV_HEADS, seq, HEAD_DIM), BF16),
            jax.ShapeDtypeStruct((batch, KV_DIM, seq), BF16),
        ],
        compiler_params=_params("parallel"),
        name="qkv_rope",
    )(x, gain, w, qg, kg, cos, sin, e1, e2)


def _attn_kernel(qT_ref, k_ref, vT_ref, oT_ref, s0_ref, s1_ref, m_ref, acc_ref, *, seq):
    n_chunks = seq // TK
    ones = jnp.ones((SUM_ROWS, TK), BF16)

    def scores(c, s_ref):
        q = jnp.concatenate(
            [qT_ref[0, g * HEAD_DIM:(g + 1) * HEAD_DIM, :] for g in range(GQA_GROUP)], axis=1)
        kc = k_ref[0, 0, pl.ds(pl.multiple_of(c * TK, TK), TK), :]
        s_ref[...] = jnp.dot(kc, q, preferred_element_type=F32)

    def softmax_pv(c, s_ref):
        s = s_ref[...]
        m_prev = m_ref[...]
        m_new = jnp.maximum(m_prev, jnp.max(s, axis=0, keepdims=True))
        alpha = jnp.exp2(m_prev - m_new)
        p = jnp.exp2(s - m_new).astype(BF16)
        vc = jnp.concatenate(
            [vT_ref[0, :, pl.ds(pl.multiple_of(c * TK, TK), TK)], ones], axis=0)
        acc_ref[...] = alpha * acc_ref[...] + jnp.dot(vc, p, preferred_element_type=F32)
        m_ref[...] = m_new

    m_ref[...] = jnp.full(m_ref.shape, -1e30, F32)
    acc_ref[...] = jnp.zeros(acc_ref.shape, F32)
    scores(0, s0_ref)

    def body(i, carry):
        c = 2 * i
        scores(c + 1, s1_ref)
        softmax_pv(c, s0_ref)
        scores(c + 2, s0_ref)
        softmax_pv(c + 1, s1_ref)
        return carry

    lax.fori_loop(0, n_chunks // 2 - 1, body, 0)
    scores(n_chunks - 1, s1_ref)
    softmax_pv(n_chunks - 2, s0_ref)
    softmax_pv(n_chunks - 1, s1_ref)
    acc = acc_ref[...]
    o = acc[:HEAD_DIM] / acc[HEAD_DIM:HEAD_DIM + 1]
    for g in range(GQA_GROUP):
        oT_ref[0, g * HEAD_DIM:(g + 1) * HEAD_DIM, :] = o[:, g * TQ:(g + 1) * TQ].astype(BF16)


def _attn_call(qT, k, vT, batch, seq):
    grp = GQA_GROUP * HEAD_DIM
    lanes = GQA_GROUP * TQ
    return pl.pallas_call(
        functools.partial(_attn_kernel, seq=seq),
        grid=(batch, N_KV_HEADS, seq // TQ),
        in_specs=[
            pl.BlockSpec((1, grp, TQ), lambda b, h, i: (b, h, i)),
            pl.BlockSpec((1, 1, seq, HEAD_DIM), lambda b, h, i: (b, h, 0, 0)),
            pl.BlockSpec((1, HEAD_DIM, seq), lambda b, h, i: (b, h, 0)),
        ],
        out_specs=pl.BlockSpec((1, grp, TQ), lambda b, h, i: (b, h, i)),
        out_shape=jax.ShapeDtypeStruct((batch, Q_DIM, seq), BF16),
        scratch_shapes=[
            pltpu.VMEM((TK, lanes), F32),
            pltpu.VMEM((TK, lanes), F32),
            pltpu.VMEM((1, lanes), F32),
            pltpu.VMEM((HEAD_DIM + SUM_ROWS, lanes), F32),
        ],
        compiler_params=_params("parallel", "parallel", "parallel"),
        name="axial_attention",
    )(qT, k, vT)


def _memkv_kernel(mem_ref, g_ref, w_ref, kT_ref, v_ref):
    h = _rms(mem_ref[0], g_ref[0])
    kv = jnp.dot(h.astype(BF16), w_ref[0], preferred_element_type=F32)
    kT_ref[0, 0] = kv[:, :D_MODEL].T.astype(BF16)
    v_ref[0, 0] = kv[:, D_MODEL:].astype(BF16)


def _memkv_call(mem, gain, w, depth, batch):
    return pl.pallas_call(
        _memkv_kernel,
        grid=(depth, batch),
        in_specs=[
            pl.BlockSpec((1, MEM_LEN, D_MODEL), lambda l, b: (b, 0, 0)),
            pl.BlockSpec((1, 1, D_MODEL), lambda l, b: (l, 0, 0)),
            pl.BlockSpec((1, D_MODEL, 2 * D_MODEL), lambda l, b: (l, 0, 0)),
        ],
        out_specs=[
            pl.BlockSpec((1, 1, D_MODEL, MEM_LEN), lambda l, b: (l, b, 0, 0)),
            pl.BlockSpec((1, 1, MEM_LEN, D_MODEL), lambda l, b: (l, b, 0, 0)),
        ],
        out_shape=[
            jax.ShapeDtypeStruct((depth, batch, D_MODEL, MEM_LEN), BF16),
            jax.ShapeDtypeStruct((depth, batch, MEM_LEN, D_MODEL), BF16),
        ],
        compiler_params=_params("parallel", "parallel"),
        name="memory_kv",
    )(mem, gain, w)


def _xattn(x1, gx_ref, wq_ref, kT_ref, v_ref, wo_ref):
    h = _rms(x1, gx_ref[...])
    q = jnp.dot(h.astype(BF16), wq_ref[...], preferred_element_type=F32)
    q = q * ((X_HEAD_DIM ** -0.5) * LOG2E)
    outs = []
    for hd in range(X_HEADS):
        sl = slice(hd * X_HEAD_DIM, (hd + 1) * X_HEAD_DIM)
        s = jnp.dot(q[:, sl].astype(BF16), kT_ref[0, 0, sl, :],
                    preferred_element_type=F32)
        p = jnp.exp2(s - jnp.max(s, axis=-1, keepdims=True))
        l = jnp.sum(p, axis=-1, keepdims=True)
        o = jnp.dot(p.astype(BF16), v_ref[0, 0, :, sl], preferred_element_type=F32)
        outs.append((o / l).astype(BF16))
    o = jnp.concatenate(outs, axis=1)
    return x1 + jnp.dot(o, wo_ref[...], preferred_element_type=F32)


def _attn_out_xattn_kernel(x_ref, oT_ref, wo_ref, gx_ref, wq_ref, kT_ref, v_ref, wo2_ref,
                           out_ref):
    o = oT_ref[0].T
    x1 = x_ref[...] + jnp.dot(o, wo_ref[...], preferred_element_type=F32)
    out_ref[...] = _xattn(x1, gx_ref, wq_ref, kT_ref, v_ref, wo2_ref)


def _xattn_specs(layer, nt):
    return [
        _const_spec((1, D_MODEL)),
        _const_spec((D_MODEL, D_MODEL)),
        pl.BlockSpec((1, 1, D_MODEL, MEM_LEN), lambda i: (layer, i // nt, 0, 0)),
        pl.BlockSpec((1, 1, MEM_LEN, D_MODEL), lambda i: (layer, i // nt, 0, 0)),
        _const_spec((D_MODEL, D_MODEL)),
    ]


def _attn_out_xattn_call(x, oT, wo, gx, wq, kT, v, wo2, layer, batch, seq):
    nt = seq // TM
    return pl.pallas_call(
        _attn_out_xattn_kernel,
        grid=(batch * nt,),
        in_specs=[
            pl.BlockSpec((TM, D_MODEL), lambda i: (i, 0)),
            pl.BlockSpec((1, Q_DIM, TM), lambda i: (i // nt, 0, i % nt)),
            _const_spec((Q_DIM, D_MODEL)),
        ] + _xattn_specs(layer, nt),
        out_specs=pl.BlockSpec((TM, D_MODEL), lambda i: (i, 0)),
        out_shape=jax.ShapeDtypeStruct(x.shape, F32),
        compiler_params=_params("parallel"),
        name="attn_out_xattn",
    )(x, oT, wo, gx, wq, kT, v, wo2)


def _normed_with_halo(xp_ref, x_ref, xn_ref, gain, nt):
    i = pl.program_id(0)
    keep_prev = (i % nt != 0).astype(F32)
    keep_next = (i % nt != nt - 1).astype(F32)
    hp = _rms(xp_ref[...], gain) * keep_prev
    hn = _rms(xn_ref[...], gain) * keep_next
    return jnp.concatenate([hp, _rms(x_ref[...], gain), hn], axis=0)


def _halo_in_specs(n_tiles):
    blocks = TM // HALO
    n_halo_blocks = n_tiles * blocks
    return [
        pl.BlockSpec((HALO, D_MODEL), lambda i: (jnp.maximum(i * blocks - 1, 0), 0)),
        pl.BlockSpec((TM, D_MODEL), lambda i: (i, 0)),
        pl.BlockSpec((HALO, D_MODEL),
                     lambda i: (jnp.minimum((i + 1) * blocks, n_halo_blocks - 1), 0)),
    ]


def _pool_xattn_kernel(xp_ref, x_ref, xn_ref, gp_ref, wp_ref, sc_ref,
                       gx_ref, wq_ref, kT_ref, v_ref, wo2_ref, out_ref, *, nt, seq):
    i = pl.program_id(0)
    rows = TM + 2 * HALO
    h = _normed_with_halo(xp_ref, x_ref, xn_ref, gp_ref[...], nt)
    pos = (i % nt) * TM + lax.broadcasted_iota(jnp.int32, (TM, 1), 0)
    ys = []
    for g, w in enumerate(POOL_WINDOWS):
        sl = slice(g * POOL_GROUP_W, (g + 1) * POOL_GROUP_W)
        hg = h[:, sl]
        acc = hg
        span = 1
        while span < w:
            acc = acc + pltpu.roll(acc, span, 0)
            span *= 2
        shift = w // 2 - 1
        if shift:
            acc = pltpu.roll(acc, rows - shift, 0)
        cnt = (jnp.minimum(pos + w // 2, seq) - jnp.maximum(pos - w // 2, 0)).astype(F32)
        mixed = acc[HALO:HALO + TM] / cnt - hg[HALO:HALO + TM]
        ys.append(jnp.dot(mixed.astype(BF16), wp_ref[g], preferred_element_type=F32))
    y = jnp.concatenate(ys, axis=1) * sc_ref[...]
    x1 = x_ref[...] + y
    out_ref[...] = _xattn(x1, gx_ref, wq_ref, kT_ref, v_ref, wo2_ref)


def _pool_xattn_call(x, gp, wp, sc, gx, wq, kT, v, wo2, layer, batch, seq):
    nt = seq // TM
    return pl.pallas_call(
        functools.partial(_pool_xattn_kernel, nt=nt, seq=seq),
        grid=(batch * nt,),
        in_specs=_halo_in_specs(batch * nt) + [
            _const_spec((1, D_MODEL)),
            _const_spec((len(POOL_WINDOWS), POOL_GROUP_W, POOL_GROUP_W)),
            _const_spec((1, D_MODEL)),
        ] + _xattn_specs(layer, nt),
        out_specs=pl.BlockSpec((TM, D_MODEL), lambda i: (i, 0)),
        out_shape=jax.ShapeDtypeStruct(x.shape, F32),
        compiler_params=_params("parallel"),
        name="pool_xattn",
    )(x, x, x, gp, wp, sc, gx, wq, kT, v, wo2)


def _ffn_kernel(xp_ref, x_ref, xn_ref, g_ref, wup_ref, cw_ref, cb_ref, wdn_ref, gf_ref,
                out_ref, h_ref, act_ref, *, nt, final_norm):
    rows = TM + 2 * HALO
    h_ref[...] = _normed_with_halo(xp_ref, x_ref, xn_ref, g_ref[...], nt).astype(BF16)

    def conv_up(off):
        u = jnp.dot(h_ref[...], wup_ref[:, off:off + FF_CHUNK], preferred_element_type=F32)
        cw = cw_ref[:, off:off + FF_CHUNK]
        c = (pltpu.roll(u, 1, 0) * cw[0:1] + u * cw[1:2]
             + pltpu.roll(u, rows - 1, 0) * cw[2:3] + cb_ref[:, off:off + FF_CHUNK])
        return c[HALO:HALO + TM]

    for c in range(D_FF // FF_CHUNK):
        gate = conv_up(c * FF_CHUNK)
        val = conv_up(D_FF + c * FF_CHUNK)
        act = gate * (1.0 / (1.0 + jnp.exp(-gate))) * val
        act_ref[:, c * FF_CHUNK:(c + 1) * FF_CHUNK] = act.astype(BF16)
    y = x_ref[...] + jnp.dot(act_ref[...], wdn_ref[...], preferred_element_type=F32)
    if final_norm:
        y = _rms(y, gf_ref[...])
    out_ref[...] = y


def _ffn_call(x, gain, wup, cw, cb, wdn, gf, final_norm, batch, seq):
    nt = seq // TM
    return pl.pallas_call(
        functools.partial(_ffn_kernel, nt=nt, final_norm=final_norm),
        grid=(batch * nt,),
        in_specs=_halo_in_specs(batch * nt) + [
            _const_spec((1, D_MODEL)),
            _const_spec((D_MODEL, 2 * D_FF)),
            _const_spec((3, 2 * D_FF)),
            _const_spec((1, 2 * D_FF)),
            _const_spec((D_FF, D_MODEL)),
            _const_spec((1, D_MODEL)),
        ],
        out_specs=pl.BlockSpec((TM, D_MODEL), lambda i: (i, 0)),
        out_shape=jax.ShapeDtypeStruct(x.shape, F32),
        scratch_shapes=[
            pltpu.VMEM((TM + 2 * HALO, D_MODEL), BF16),
            pltpu.VMEM((TM, D_FF), BF16),
        ],
        compiler_params=_params("parallel"),
        name="conv_ffn",
    )(x, x, x, gain, wup, cw, cb, wdn, gf)


def _rope_tables(seq):
    t = jnp.arange(seq, dtype=jnp.int32)
    row = (t // GRID_W).astype(F32)
    col = (t % GRID_W).astype(F32)
    inv_freq = ROPE_THETA ** (-jnp.arange(ROPE_PAIRS, dtype=F32) / ROPE_PAIRS)
    ang_row = row[:, None] * inv_freq
    ang_col = col[:, None] * inv_freq
    ang = jnp.concatenate([ang_row, ang_row, ang_col, ang_col], axis=1)
    sign = jnp.concatenate([-jnp.ones((ROPE_PAIRS,), F32), jnp.ones((ROPE_PAIRS,), F32)] * 2)
    cos = jnp.tile(jnp.cos(ang), (1, 2))
    sin = jnp.tile(jnp.sin(ang) * sign, (1, 2))
    return cos, sin


def _head_indicators():
    col_head = jnp.arange(QK_DIM) // HEAD_DIM
    e = (col_head[:, None] == jnp.arange(LANES)[None, :]).astype(F32)
    e1 = (e / HEAD_DIM).astype(BF16)
    e2 = jnp.concatenate([e.T, e.T], axis=0).astype(BF16)
    return e1, e2


def kernel(x, mem, attn_norm, attn_w_qkv, attn_q_gain, attn_k_gain, attn_w_o, pool_norm, pool_w, pool_scale, xattn_norm, mem_norm, xattn_w_q, xattn_w_kv, xattn_w_o, ffn_norm, ffn_w_up, ffn_conv_w, ffn_conv_b, ffn_w_down, final_norm):
    batch, seq, d = x.shape
    depth = xattn_norm.shape[0]
    assert d == D_MODEL and seq % TM == 0 and seq % TQ == 0 and seq % TK == 0
    xf = x.reshape(batch * seq, d)

    cos, sin = _rope_tables(seq)
    e1, e2 = _head_indicators()
    kT_mem, v_mem = _memkv_call(mem, mem_norm.reshape(depth, 1, d),
                                xattn_w_kv.astype(BF16), depth, batch)

    ia = ib = 0
    for i in range(depth):
        xa = (xattn_norm[i].reshape(1, d), xattn_w_q[i].astype(BF16), kT_mem, v_mem,
              xattn_w_o[i].astype(BF16))
        if i % 2 == 0:
            qT, k, vT = _qkv_call(
                xf, attn_norm[ia].reshape(1, d), attn_w_qkv[ia].astype(BF16),
                jnp.tile(attn_q_gain[ia], 2).reshape(1, LANES),
                jnp.tile(attn_k_gain[ia], 2).reshape(1, LANES),
                cos, sin, e1, e2, batch, seq)
            oT = _attn_call(qT, k, vT, batch, seq)
            xf = _attn_out_xattn_call(xf, oT, attn_w_o[ia].astype(BF16), *xa, i, batch, seq)
            ia += 1
        else:
            xf = _pool_xattn_call(xf, pool_norm[ib].reshape(1, d), pool_w[ib].astype(BF16),
                                  pool_scale[ib].reshape(1, d), *xa, i, batch, seq)
            ib += 1
        xf = _ffn_call(xf, ffn_norm[i].reshape(1, d), ffn_w_up[i].astype(BF16),
                       ffn_conv_w[i], ffn_conv_b[i].reshape(1, 2 * D_FF),
                       ffn_w_down[i].astype(BF16), final_norm.reshape(1, d),
                       i == depth - 1, batch, seq)
    return xf.reshape(batch, seq, d)
```

```python
import functools
import math

import jax
import jax.numpy as jnp
from jax import lax
from jax.experimental import pallas as pl
from jax.experimental.pallas import tpu as pltpu

F32 = jnp.float32
BF16 = jnp.bfloat16

D_MODEL = 1024
GRID_W = 64
EPS = 1e-6
HEAD_DIM = 64
N_HEADS = 16
N_KV_HEADS = 4
GQA_GROUP = 4
ROPE_THETA = 10000.0
ROPE_PAIRS = 16
Q_DIM = N_HEADS * HEAD_DIM
KV_DIM = N_KV_HEADS * HEAD_DIM
QK_DIM = Q_DIM + KV_DIM
QKV_DIM = Q_DIM + 2 * KV_DIM
POOL_WINDOWS = (2, 4, 8, 16)
POOL_GROUP_W = 256
MEM_LEN = 256
X_HEADS = 4
X_HEAD_DIM = 256
D_FF = 2816
LOG2E = math.log2(math.e)

LANES = 128
HALO = 16
SUM_ROWS = 16
VMEM_LIMIT = 56 * 1024 * 1024

TM = 512
TM_FFN = 1024
TQ = 256
TK = 256
QK_ROWS = 2 * HEAD_DIM
MAX_STATIC_SHIFT = 40.0
FF_CHUNK = 256


def _rms(x, gain):
    return x * lax.rsqrt(jnp.mean(x * x, axis=-1, keepdims=True) + EPS) * gain


def _params(*sem):
    return pltpu.CompilerParams(dimension_semantics=sem, vmem_limit_bytes=VMEM_LIMIT)


def _const_spec(shape):
    nd = len(shape)
    return pl.BlockSpec(shape, lambda *_: (0,) * nd)


def _qkv_kernel(x_ref, g_ref, w_ref, qg_ref, kg_ref, cos_ref, sin_ref, e1_ref, e2_ref,
                qx_ref, kx_ref, qT_ref, k_ref, vT_ref):
    h = _rms(x_ref[...], g_ref[...])
    y = jnp.dot(h.astype(BF16), w_ref[...], preferred_element_type=F32)
    qk = y[:, :QK_DIM]
    ms = jnp.dot((qk * qk).astype(BF16), e1_ref[...], preferred_element_type=F32)
    r = lax.rsqrt(ms + EPS)
    head = lax.broadcasted_iota(jnp.int32, (1, LANES), 1)
    r = r * jnp.where(head < N_HEADS, (HEAD_DIM ** -0.5) * LOG2E, 1.0)
    r_hi = r.astype(BF16)
    r_lo = (r - r_hi.astype(F32)).astype(BF16)
    rb = jnp.dot(jnp.concatenate([r_hi, r_lo], axis=1), e2_ref[...],
                 preferred_element_type=F32)
    cos = cos_ref[...]
    sin = sin_ref[...]
    lane = lax.broadcasted_iota(jnp.int32, cos.shape, 1)
    first = (lane % 32) < ROPE_PAIRS
    outs = []
    for j in range(QK_DIM // LANES):
        sl = slice(j * LANES, (j + 1) * LANES)
        yb = qk[:, sl] * (qg_ref[...] if j < Q_DIM // LANES else kg_ref[...])
        partner = jnp.where(first, pltpu.roll(yb, LANES - ROPE_PAIRS, 1),
                            pltpu.roll(yb, ROPE_PAIRS, 1))
        outs.append((yb * cos + partner * sin) * rb[:, sl])
    q = jnp.concatenate(outs[:Q_DIM // LANES], axis=1)
    k = jnp.concatenate(outs[Q_DIM // LANES:], axis=1)
    qT = q.T.astype(BF16)
    for hd in range(N_HEADS):
        qT_ref[0, hd, :HEAD_DIM, :] = qT[hd * HEAD_DIM:(hd + 1) * HEAD_DIM]
        qT_ref[0, hd, HEAD_DIM:, :] = qx_ref[...]
    vT = y[:, QK_DIM:].T.astype(BF16)
    ones = jnp.ones((SUM_ROWS, vT.shape[1]), BF16)
    for g in range(N_KV_HEADS):
        sl = slice(g * HEAD_DIM, (g + 1) * HEAD_DIM)
        k_ref[0, g] = jnp.concatenate([k[:, sl].astype(BF16), kx_ref[...]], axis=1)
        vT_ref[0, g] = jnp.concatenate([vT[sl], ones], axis=0)


def _qkv_call(x, gain, w, qg, kg, cos, sin, e1, e2, qx, kx, batch, seq):
    nt = seq // TM
    v_rows = HEAD_DIM + SUM_ROWS
    return pl.pallas_call(
        _qkv_kernel,
        grid=(batch * nt,),
        in_specs=[
            pl.BlockSpec((TM, D_MODEL), lambda i: (i, 0)),
            _const_spec((1, D_MODEL)),
            _const_spec((D_MODEL, QKV_DIM)),
            _const_spec((1, LANES)),
            _const_spec((1, LANES)),
            pl.BlockSpec((TM, LANES), lambda i: (i % nt, 0)),
            pl.BlockSpec((TM, LANES), lambda i: (i % nt, 0)),
            _const_spec((QK_DIM, LANES)),
            _const_spec((2 * LANES, QK_DIM)),
            _const_spec((QK_ROWS - HEAD_DIM, TM)),
            _const_spec((TM, QK_ROWS - HEAD_DIM)),
        ],
        out_specs=[
            pl.BlockSpec((1, N_HEADS, QK_ROWS, TM), lambda i: (i // nt, 0, 0, i % nt)),
            pl.BlockSpec((1, N_KV_HEADS, TM, QK_ROWS), lambda i: (i // nt, 0, i % nt, 0)),
            pl.BlockSpec((1, N_KV_HEADS, v_rows, TM), lambda i: (i // nt, 0, 0, i % nt)),
        ],
        out_shape=[
            jax.ShapeDtypeStruct((batch, N_HEADS, QK_ROWS, seq), BF16),
            jax.ShapeDtypeStruct((batch, N_KV_HEADS, seq, QK_ROWS), BF16),
            jax.ShapeDtypeStruct((batch, N_KV_HEADS, v_rows, seq), BF16),
        ],
        compiler_params=_params("parallel"),
        name="qkv_rope",
    )(x, gain, w, qg, kg, cos, sin, e1, e2, qx, kx)


def _attn_fixed_kernel(qT_ref, k_ref, vT_ref, oT_ref):
    q = jnp.concatenate([qT_ref[0, g] for g in range(GQA_GROUP)], axis=1)
    s = jnp.dot(k_ref[0, 0], q, preferred_element_type=F32)
    p = jnp.exp2(s).astype(BF16)
    o = jnp.dot(vT_ref[0, 0], p, preferred_element_type=F32)
    o = o[:HEAD_DIM] / o[HEAD_DIM:HEAD_DIM + 1]
    for g in range(GQA_GROUP):
        oT_ref[0, g * HEAD_DIM:(g + 1) * HEAD_DIM, :] = o[:, g * TQ:(g + 1) * TQ].astype(BF16)


def _attn_online_kernel(qT_ref, k_ref, vT_ref, oT_ref, s0_ref, s1_ref, m_ref, acc_ref, *, seq):
    n_chunks = seq // TK

    def scores(c, s_ref):
        q = jnp.concatenate([qT_ref[0, g] for g in range(GQA_GROUP)], axis=1)
        kc = k_ref[0, 0, pl.ds(pl.multiple_of(c * TK, TK), TK), :]
        s_ref[...] = jnp.dot(kc, q, preferred_element_type=F32)

    def softmax_pv(c, s_ref):
        s = s_ref[...]
        m_prev = m_ref[...]
        m_new = jnp.maximum(m_prev, jnp.max(s, axis=0, keepdims=True))
        alpha = jnp.exp2(m_prev - m_new)
        p = jnp.exp2(s - m_new).astype(BF16)
        vc = vT_ref[0, 0, :, pl.ds(pl.multiple_of(c * TK, TK), TK)]
        acc_ref[...] = alpha * acc_ref[...] + jnp.dot(vc, p, preferred_element_type=F32)
        m_ref[...] = m_new

    m_ref[...] = jnp.full(m_ref.shape, -1e30, F32)
    acc_ref[...] = jnp.zeros(acc_ref.shape, F32)
    scores(0, s0_ref)

    def body(i, carry):
        c = 2 * i
        scores(c + 1, s1_ref)
        softmax_pv(c, s0_ref)
        scores(c + 2, s0_ref)
        softmax_pv(c + 1, s1_ref)
        return carry

    lax.fori_loop(0, n_chunks // 2 - 1, body, 0)
    scores(n_chunks - 1, s1_ref)
    softmax_pv(n_chunks - 2, s0_ref)
    softmax_pv(n_chunks - 1, s1_ref)
    acc = acc_ref[...]
    o = acc[:HEAD_DIM] / acc[HEAD_DIM:HEAD_DIM + 1]
    for g in range(GQA_GROUP):
        oT_ref[0, g * HEAD_DIM:(g + 1) * HEAD_DIM, :] = o[:, g * TQ:(g + 1) * TQ].astype(BF16)


def _attn_call(qT, k, vT, batch, seq, online):
    grp = GQA_GROUP * HEAD_DIM
    lanes = GQA_GROUP * TQ
    v_rows = HEAD_DIM + SUM_ROWS
    if online:
        body = functools.partial(_attn_online_kernel, seq=seq)
        scratch = [
            pltpu.VMEM((TK, lanes), F32),
            pltpu.VMEM((TK, lanes), F32),
            pltpu.VMEM((1, lanes), F32),
            pltpu.VMEM((v_rows, lanes), F32),
        ]
    else:
        body, scratch = _attn_fixed_kernel, []
    return pl.pallas_call(
        body,
        grid=(batch, N_KV_HEADS, seq // TQ),
        in_specs=[
            pl.BlockSpec((1, GQA_GROUP, QK_ROWS, TQ), lambda b, h, i: (b, h, 0, i)),
            pl.BlockSpec((1, 1, seq, QK_ROWS), lambda b, h, i: (b, h, 0, 0)),
            pl.BlockSpec((1, 1, v_rows, seq), lambda b, h, i: (b, h, 0, 0)),
        ],
        out_specs=pl.BlockSpec((1, grp, TQ), lambda b, h, i: (b, h, i)),
        out_shape=jax.ShapeDtypeStruct((batch, Q_DIM, seq), BF16),
        scratch_shapes=scratch,
        compiler_params=_params("parallel", "parallel", "parallel"),
        name="axial_attention_online" if online else "axial_attention",
    )(qT, k, vT)


def _memkv_kernel(mem_ref, g_ref, w_ref, kT_ref, v_ref):
    h = _rms(mem_ref[0], g_ref[0])
    kv = jnp.dot(h.astype(BF16), w_ref[0], preferred_element_type=F32)
    kT_ref[0, 0] = kv[:, :D_MODEL].T.astype(BF16)
    v_ref[0, 0] = kv[:, D_MODEL:].astype(BF16)


def _memkv_call(mem, gain, w, depth, batch):
    return pl.pallas_call(
        _memkv_kernel,
        grid=(depth, batch),
        in_specs=[
            pl.BlockSpec((1, MEM_LEN, D_MODEL), lambda l, b: (b, 0, 0)),
            pl.BlockSpec((1, 1, D_MODEL), lambda l, b: (l, 0, 0)),
            pl.BlockSpec((1, D_MODEL, 2 * D_MODEL), lambda l, b: (l, 0, 0)),
        ],
        out_specs=[
            pl.BlockSpec((1, 1, D_MODEL, MEM_LEN), lambda l, b: (l, b, 0, 0)),
            pl.BlockSpec((1, 1, MEM_LEN, D_MODEL), lambda l, b: (l, b, 0, 0)),
        ],
        out_shape=[
            jax.ShapeDtypeStruct((depth, batch, D_MODEL, MEM_LEN), BF16),
            jax.ShapeDtypeStruct((depth, batch, MEM_LEN, D_MODEL), BF16),
        ],
        compiler_params=_params("parallel", "parallel"),
        name="memory_kv",
    )(mem, gain, w)


def _xattn(x1, gx_ref, wq_ref, kT_ref, v_ref, wo_ref):
    h = _rms(x1, gx_ref[...])
    q = jnp.dot(h.astype(BF16), wq_ref[...], preferred_element_type=F32)
    q = q * ((X_HEAD_DIM ** -0.5) * LOG2E)
    outs = []
    for hd in range(X_HEADS):
        sl = slice(hd * X_HEAD_DIM, (hd + 1) * X_HEAD_DIM)
        s = jnp.dot(q[:, sl].astype(BF16), kT_ref[0, 0, sl, :],
                    preferred_element_type=F32)
        p = jnp.exp2(s - jnp.max(s, axis=-1, keepdims=True))
        l = jnp.sum(p, axis=-1, keepdims=True)
        o = jnp.dot(p.astype(BF16), v_ref[0, 0, :, sl], preferred_element_type=F32)
        outs.append((o / l).astype(BF16))
    o = jnp.concatenate(outs, axis=1)
    return x1 + jnp.dot(o, wo_ref[...], preferred_element_type=F32)


def _attn_out_xattn_kernel(x_ref, oT_ref, wo_ref, gx_ref, wq_ref, kT_ref, v_ref, wo2_ref,
                           out_ref):
    o = oT_ref[0].T
    x1 = x_ref[...] + jnp.dot(o, wo_ref[...], preferred_element_type=F32)
    out_ref[...] = _xattn(x1, gx_ref, wq_ref, kT_ref, v_ref, wo2_ref)


def _xattn_specs(layer, nt):
    return [
        _const_spec((1, D_MODEL)),
        _const_spec((D_MODEL, D_MODEL)),
        pl.BlockSpec((1, 1, D_MODEL, MEM_LEN), lambda i: (layer, i // nt, 0, 0)),
        pl.BlockSpec((1, 1, MEM_LEN, D_MODEL), lambda i: (layer, i // nt, 0, 0)),
        _const_spec((D_MODEL, D_MODEL)),
    ]


def _attn_out_xattn_call(x, oT, wo, gx, wq, kT, v, wo2, layer, batch, seq):
    nt = seq // TM
    return pl.pallas_call(
        _attn_out_xattn_kernel,
        grid=(batch * nt,),
        in_specs=[
            pl.BlockSpec((TM, D_MODEL), lambda i: (i, 0)),
            pl.BlockSpec((1, Q_DIM, TM), lambda i: (i // nt, 0, i % nt)),
            _const_spec((Q_DIM, D_MODEL)),
        ] + _xattn_specs(layer, nt),
        out_specs=pl.BlockSpec((TM, D_MODEL), lambda i: (i, 0)),
        out_shape=jax.ShapeDtypeStruct(x.shape, F32),
        compiler_params=_params("parallel"),
        name="attn_out_xattn",
    )(x, oT, wo, gx, wq, kT, v, wo2)


def _normed_with_halo(xp_ref, x_ref, xn_ref, gain, nt):
    i = pl.program_id(0)
    keep_prev = (i % nt != 0).astype(F32)
    keep_next = (i % nt != nt - 1).astype(F32)
    hp = _rms(xp_ref[...], gain) * keep_prev
    hn = _rms(xn_ref[...], gain) * keep_next
    return jnp.concatenate([hp, _rms(x_ref[...], gain), hn], axis=0)


def _halo_in_specs(n_tiles, tm):
    blocks = tm // HALO
    n_halo_blocks = n_tiles * blocks
    return [
        pl.BlockSpec((HALO, D_MODEL), lambda i: (jnp.maximum(i * blocks - 1, 0), 0)),
        pl.BlockSpec((tm, D_MODEL), lambda i: (i, 0)),
        pl.BlockSpec((HALO, D_MODEL),
                     lambda i: (jnp.minimum((i + 1) * blocks, n_halo_blocks - 1), 0)),
    ]


def _pool_xattn_kernel(xp_ref, x_ref, xn_ref, gp_ref, wp_ref, sc_ref,
                       gx_ref, wq_ref, kT_ref, v_ref, wo2_ref, out_ref, *, nt, seq):
    i = pl.program_id(0)
    rows = TM + 2 * HALO
    h = _normed_with_halo(xp_ref, x_ref, xn_ref, gp_ref[...], nt)
    pos = (i % nt) * TM + lax.broadcasted_iota(jnp.int32, (TM, 1), 0)
    ys = []
    for g, w in enumerate(POOL_WINDOWS):
        sl = slice(g * POOL_GROUP_W, (g + 1) * POOL_GROUP_W)
        hg = h[:, sl]
        acc = hg
        span = 1
        while span < w:
            acc = acc + pltpu.roll(acc, span, 0)
            span *= 2
        shift = w // 2 - 1
        if shift:
            acc = pltpu.roll(acc, rows - shift, 0)
        cnt = (jnp.minimum(pos + w // 2, seq) - jnp.maximum(pos - w // 2, 0)).astype(F32)
        mixed = acc[HALO:HALO + TM] / cnt - hg[HALO:HALO + TM]
        ys.append(jnp.dot(mixed.astype(BF16), wp_ref[g], preferred_element_type=F32))
    y = jnp.concatenate(ys, axis=1) * sc_ref[...]
    x1 = x_ref[...] + y
    out_ref[...] = _xattn(x1, gx_ref, wq_ref, kT_ref, v_ref, wo2_ref)


def _pool_xattn_call(x, gp, wp, sc, gx, wq, kT, v, wo2, layer, batch, seq):
    nt = seq // TM
    return pl.pallas_call(
        functools.partial(_pool_xattn_kernel, nt=nt, seq=seq),
        grid=(batch * nt,),
        in_specs=_halo_in_specs(batch * nt, TM) + [
            _const_spec((1, D_MODEL)),
            _const_spec((len(POOL_WINDOWS), POOL_GROUP_W, POOL_GROUP_W)),
            _const_spec((1, D_MODEL)),
        ] + _xattn_specs(layer, nt),
        out_specs=pl.BlockSpec((TM, D_MODEL), lambda i: (i, 0)),
        out_shape=jax.ShapeDtypeStruct(x.shape, F32),
        compiler_params=_params("parallel"),
        name="pool_xattn",
    )(x, x, x, gp, wp, sc, gx, wq, kT, v, wo2)


def _ffn_kernel(xp_ref, x_ref, xn_ref, g_ref, wup_ref, cw_ref, cb_ref, wdn_ref, gf_ref,
                out_ref, h_ref, act_ref, *, nt, final_norm):
    tm = x_ref.shape[0]
    rows = tm + 2 * HALO
    h_ref[...] = _normed_with_halo(xp_ref, x_ref, xn_ref, g_ref[...], nt).astype(BF16)

    def conv_up(off):
        u = jnp.dot(h_ref[...], wup_ref[:, off:off + FF_CHUNK], preferred_element_type=F32)
        cw = cw_ref[:, off:off + FF_CHUNK]
        c = (pltpu.roll(u, 1, 0) * cw[0:1] + u * cw[1:2]
             + pltpu.roll(u, rows - 1, 0) * cw[2:3] + cb_ref[:, off:off + FF_CHUNK])
        return c[HALO:HALO + tm]

    for c in range(D_FF // FF_CHUNK):
        gate = conv_up(c * FF_CHUNK)
        val = conv_up(D_FF + c * FF_CHUNK)
        act = gate * (1.0 / (1.0 + jnp.exp(-gate))) * val
        act_ref[:, c * FF_CHUNK:(c + 1) * FF_CHUNK] = act.astype(BF16)
    y = x_ref[...] + jnp.dot(act_ref[...], wdn_ref[...], preferred_element_type=F32)
    if final_norm:
        y = _rms(y, gf_ref[...])
    out_ref[...] = y


def _ffn_call(x, gain, wup, cw, cb, wdn, gf, final_norm, batch, seq):
    nt = seq // TM_FFN
    return pl.pallas_call(
        functools.partial(_ffn_kernel, nt=nt, final_norm=final_norm),
        grid=(batch * nt,),
        in_specs=_halo_in_specs(batch * nt, TM_FFN) + [
            _const_spec((1, D_MODEL)),
            _const_spec((D_MODEL, 2 * D_FF)),
            _const_spec((3, 2 * D_FF)),
            _const_spec((1, 2 * D_FF)),
            _const_spec((D_FF, D_MODEL)),
            _const_spec((1, D_MODEL)),
        ],
        out_specs=pl.BlockSpec((TM_FFN, D_MODEL), lambda i: (i, 0)),
        out_shape=jax.ShapeDtypeStruct(x.shape, F32),
        scratch_shapes=[
            pltpu.VMEM((TM_FFN + 2 * HALO, D_MODEL), BF16),
            pltpu.VMEM((TM_FFN, D_FF), BF16),
        ],
        compiler_params=_params("parallel"),
        name="conv_ffn",
    )(x, x, x, gain, wup, cw, cb, wdn, gf)


def _rope_tables(seq):
    t = jnp.arange(seq, dtype=jnp.int32)
    row = (t // GRID_W).astype(F32)
    col = (t % GRID_W).astype(F32)
    inv_freq = ROPE_THETA ** (-jnp.arange(ROPE_PAIRS, dtype=F32) / ROPE_PAIRS)
    ang_row = row[:, None] * inv_freq
    ang_col = col[:, None] * inv_freq
    ang = jnp.concatenate([ang_row, ang_row, ang_col, ang_col], axis=1)
    sign = jnp.concatenate([-jnp.ones((ROPE_PAIRS,), F32), jnp.ones((ROPE_PAIRS,), F32)] * 2)
    cos = jnp.tile(jnp.cos(ang), (1, 2))
    sin = jnp.tile(jnp.sin(ang) * sign, (1, 2))
    return cos, sin


def _head_indicators():
    col_head = jnp.arange(QK_DIM) // HEAD_DIM
    e = (col_head[:, None] == jnp.arange(LANES)[None, :]).astype(F32)
    e1 = (e / HEAD_DIM).astype(BF16)
    e2 = jnp.concatenate([e.T, e.T], axis=0).astype(BF16)
    return e1, e2


def kernel(x, mem, attn_norm, attn_w_qkv, attn_q_gain, attn_k_gain, attn_w_o, pool_norm, pool_w, pool_scale, xattn_norm, mem_norm, xattn_w_q, xattn_w_kv, xattn_w_o, ffn_norm, ffn_w_up, ffn_conv_w, ffn_conv_b, ffn_w_down, final_norm):
    batch, seq, d = x.shape
    depth = xattn_norm.shape[0]
    assert d == D_MODEL and seq % TM == 0 and seq % TM_FFN == 0 and seq % TQ == 0 and seq % TK == 0
    xf = x.reshape(batch * seq, d)

    cos, sin = _rope_tables(seq)
    e1, e2 = _head_indicators()
    kT_mem, v_mem = _memkv_call(mem, mem_norm.reshape(depth, 1, d),
                                xattn_w_kv.astype(BF16), depth, batch)

    ia = ib = 0
    for i in range(depth):
        xa = (xattn_norm[i].reshape(1, d), xattn_w_q[i].astype(BF16), kT_mem, v_mem,
              xattn_w_o[i].astype(BF16))
        if i % 2 == 0:
            bound = (HEAD_DIM ** 0.5) * LOG2E * jnp.max(jnp.abs(attn_q_gain[ia])) * jnp.max(
                jnp.abs(attn_k_gain[ia]))
            fixed_ok = bound <= MAX_STATIC_SHIFT
            shift = jnp.where(fixed_ok, bound, 0.0)
            qx = jnp.zeros((QK_ROWS - HEAD_DIM, TM), F32).at[0].set(-shift).astype(BF16)
            kx = jnp.zeros((TM, QK_ROWS - HEAD_DIM), BF16).at[:, 0].set(1.0)
            qT, k, vT = _qkv_call(
                xf, attn_norm[ia].reshape(1, d), attn_w_qkv[ia].astype(BF16),
                jnp.tile(attn_q_gain[ia], 2).reshape(1, LANES),
                jnp.tile(attn_k_gain[ia], 2).reshape(1, LANES),
                cos, sin, e1, e2, qx, kx, batch, seq)
            oT = lax.cond(
                fixed_ok,
                functools.partial(_attn_call, batch=batch, seq=seq, online=False),
                functools.partial(_attn_call, batch=batch, seq=seq, online=True),
                qT, k, vT)
            xf = _attn_out_xattn_call(xf, oT, attn_w_o[ia].astype(BF16), *xa, i, batch, seq)
            ia += 1
        else:
            xf = _pool_xattn_call(xf, pool_norm[ib].reshape(1, d), pool_w[ib].astype(BF16),
                                  pool_scale[ib].reshape(1, d), *xa, i, batch, seq)
            ib += 1
        xf = _ffn_call(xf, ffn_norm[i].reshape(1, d), ffn_w_up[i].astype(BF16),
                       ffn_conv_w[i], ffn_conv_b[i].reshape(1, 2 * D_FF),
                       ffn_w_down[i].astype(BF16), final_norm.reshape(1, d),
                       i == depth - 1, batch, seq)
    return xf.reshape(batch, seq, d)
```

```python
import functools
import math

import jax
import jax.numpy as jnp
from jax import lax
from jax.experimental import pallas as pl
from jax.experimental.pallas import tpu as pltpu

F32 = jnp.float32
BF16 = jnp.bfloat16

D_MODEL = 1024
GRID_W = 64
EPS = 1e-6
HEAD_DIM = 64
N_HEADS = 16
N_KV_HEADS = 4
GQA_GROUP = 4
ROPE_THETA = 10000.0
ROPE_PAIRS = 16
Q_DIM = N_HEADS * HEAD_DIM
KV_DIM = N_KV_HEADS * HEAD_DIM
QK_DIM = Q_DIM + KV_DIM
QKV_DIM = Q_DIM + 2 * KV_DIM
POOL_WINDOWS = (2, 4, 8, 16)
POOL_GROUP_W = 256
MEM_LEN = 256
X_HEADS = 4
X_HEAD_DIM = 256
D_FF = 2816
LOG2E = math.log2(math.e)

LANES = 128
HALO = 16
SUM_ROWS = 16
VMEM_LIMIT = 56 * 1024 * 1024

TM = 1024
TM_FFN = 1024
TQ = 512
TK = 256
QK_ROWS = 2 * HEAD_DIM
MAX_STATIC_SHIFT = 40.0
FF_CHUNK = 256


def _rms(x, gain):
    return x * lax.rsqrt(jnp.mean(x * x, axis=-1, keepdims=True) + EPS) * gain


def _params(*sem):
    return pltpu.CompilerParams(dimension_semantics=sem, vmem_limit_bytes=VMEM_LIMIT)


def _const_spec(shape):
    nd = len(shape)
    return pl.BlockSpec(shape, lambda *_: (0,) * nd)


def _qkv_kernel(x_ref, g_ref, w_ref, qg_ref, kg_ref, cos_ref, sin_ref, e1_ref, e2_ref,
                qx_ref, kx_ref, qT_ref, k_ref, vT_ref):
    h = _rms(x_ref[...], g_ref[...])
    y = jnp.dot(h.astype(BF16), w_ref[...], preferred_element_type=F32)
    qk = y[:, :QK_DIM]
    ms = jnp.dot((qk * qk).astype(BF16), e1_ref[...], preferred_element_type=F32)
    r = lax.rsqrt(ms + EPS)
    head = lax.broadcasted_iota(jnp.int32, (1, LANES), 1)
    r = r * jnp.where(head < N_HEADS, (HEAD_DIM ** -0.5) * LOG2E, 1.0)
    r_hi = r.astype(BF16)
    r_lo = (r - r_hi.astype(F32)).astype(BF16)
    rb = jnp.dot(jnp.concatenate([r_hi, r_lo], axis=1), e2_ref[...],
                 preferred_element_type=F32)
    cos = cos_ref[...]
    sin = sin_ref[...]
    lane = lax.broadcasted_iota(jnp.int32, cos.shape, 1)
    first = (lane % 32) < ROPE_PAIRS
    outs = []
    for j in range(QK_DIM // LANES):
        sl = slice(j * LANES, (j + 1) * LANES)
        yb = qk[:, sl] * (qg_ref[...] if j < Q_DIM // LANES else kg_ref[...])
        partner = jnp.where(first, pltpu.roll(yb, LANES - ROPE_PAIRS, 1),
                            pltpu.roll(yb, ROPE_PAIRS, 1))
        outs.append((yb * cos + partner * sin) * rb[:, sl])
    q = jnp.concatenate(outs[:Q_DIM // LANES], axis=1)
    k = jnp.concatenate(outs[Q_DIM // LANES:], axis=1)
    qT = q.T.astype(BF16)
    for hd in range(N_HEADS):
        qT_ref[0, hd, :HEAD_DIM, :] = qT[hd * HEAD_DIM:(hd + 1) * HEAD_DIM]
        qT_ref[0, hd, HEAD_DIM:, :] = qx_ref[...]
    vT = y[:, QK_DIM:].T.astype(BF16)
    ones = jnp.ones((SUM_ROWS, vT.shape[1]), BF16)
    for g in range(N_KV_HEADS):
        sl = slice(g * HEAD_DIM, (g + 1) * HEAD_DIM)
        k_ref[0, g] = jnp.concatenate([k[:, sl].astype(BF16), kx_ref[...]], axis=1)
        vT_ref[0, g] = jnp.concatenate([vT[sl], ones], axis=0)


def _qkv_call(x, gain, w, qg, kg, cos, sin, e1, e2, qx, kx, batch, seq):
    nt = seq // TM
    v_rows = HEAD_DIM + SUM_ROWS
    return pl.pallas_call(
        _qkv_kernel,
        grid=(batch * nt,),
        in_specs=[
            pl.BlockSpec((TM, D_MODEL), lambda i: (i, 0)),
            _const_spec((1, D_MODEL)),
            _const_spec((D_MODEL, QKV_DIM)),
            _const_spec((1, LANES)),
            _const_spec((1, LANES)),
            pl.BlockSpec((TM, LANES), lambda i: (i % nt, 0)),
            pl.BlockSpec((TM, LANES), lambda i: (i % nt, 0)),
            _const_spec((QK_DIM, LANES)),
            _const_spec((2 * LANES, QK_DIM)),
            _const_spec((QK_ROWS - HEAD_DIM, TM)),
            _const_spec((TM, QK_ROWS - HEAD_DIM)),
        ],
        out_specs=[
            pl.BlockSpec((1, N_HEADS, QK_ROWS, TM), lambda i: (i // nt, 0, 0, i % nt)),
            pl.BlockSpec((1, N_KV_HEADS, TM, QK_ROWS), lambda i: (i // nt, 0, i % nt, 0)),
            pl.BlockSpec((1, N_KV_HEADS, v_rows, TM), lambda i: (i // nt, 0, 0, i % nt)),
        ],
        out_shape=[
            jax.ShapeDtypeStruct((batch, N_HEADS, QK_ROWS, seq), BF16),
            jax.ShapeDtypeStruct((batch, N_KV_HEADS, seq, QK_ROWS), BF16),
            jax.ShapeDtypeStruct((batch, N_KV_HEADS, v_rows, seq), BF16),
        ],
        compiler_params=_params("parallel"),
        name="qkv_rope",
    )(x, gain, w, qg, kg, cos, sin, e1, e2, qx, kx)


def _attn_fixed_kernel(qT_ref, k_ref, vT_ref, oT_ref):
    q = jnp.concatenate([qT_ref[0, g] for g in range(GQA_GROUP)], axis=1)
    s = jnp.dot(k_ref[0, 0], q, preferred_element_type=F32)
    p = jnp.exp2(s).astype(BF16)
    o = jnp.dot(vT_ref[0, 0], p, preferred_element_type=F32)
    o = o[:HEAD_DIM] / o[HEAD_DIM:HEAD_DIM + 1]
    for g in range(GQA_GROUP):
        oT_ref[0, g * HEAD_DIM:(g + 1) * HEAD_DIM, :] = o[:, g * TQ:(g + 1) * TQ].astype(BF16)


def _attn_online_kernel(qT_ref, k_ref, vT_ref, oT_ref, s0_ref, s1_ref, m_ref, acc_ref, *, seq):
    n_chunks = seq // TK

    def scores(c, s_ref):
        q = jnp.concatenate([qT_ref[0, g] for g in range(GQA_GROUP)], axis=1)
        kc = k_ref[0, 0, pl.ds(pl.multiple_of(c * TK, TK), TK), :]
        s_ref[...] = jnp.dot(kc, q, preferred_element_type=F32)

    def softmax_pv(c, s_ref):
        s = s_ref[...]
        m_prev = m_ref[...]
        m_new = jnp.maximum(m_prev, jnp.max(s, axis=0, keepdims=True))
        alpha = jnp.exp2(m_prev - m_new)
        p = jnp.exp2(s - m_new).astype(BF16)
        vc = vT_ref[0, 0, :, pl.ds(pl.multiple_of(c * TK, TK), TK)]
        acc_ref[...] = alpha * acc_ref[...] + jnp.dot(vc, p, preferred_element_type=F32)
        m_ref[...] = m_new

    m_ref[...] = jnp.full(m_ref.shape, -1e30, F32)
    acc_ref[...] = jnp.zeros(acc_ref.shape, F32)
    scores(0, s0_ref)

    def body(i, carry):
        c = 2 * i
        scores(c + 1, s1_ref)
        softmax_pv(c, s0_ref)
        scores(c + 2, s0_ref)
        softmax_pv(c + 1, s1_ref)
        return carry

    lax.fori_loop(0, n_chunks // 2 - 1, body, 0)
    scores(n_chunks - 1, s1_ref)
    softmax_pv(n_chunks - 2, s0_ref)
    softmax_pv(n_chunks - 1, s1_ref)
    acc = acc_ref[...]
    o = acc[:HEAD_DIM] / acc[HEAD_DIM:HEAD_DIM + 1]
    for g in range(GQA_GROUP):
        oT_ref[0, g * HEAD_DIM:(g + 1) * HEAD_DIM, :] = o[:, g * TQ:(g + 1) * TQ].astype(BF16)


def _attn_call(qT, k, vT, batch, seq, online):
    grp = GQA_GROUP * HEAD_DIM
    lanes = GQA_GROUP * TQ
    v_rows = HEAD_DIM + SUM_ROWS
    if online:
        body = functools.partial(_attn_online_kernel, seq=seq)
        scratch = [
            pltpu.VMEM((TK, lanes), F32),
            pltpu.VMEM((TK, lanes), F32),
            pltpu.VMEM((1, lanes), F32),
            pltpu.VMEM((v_rows, lanes), F32),
        ]
    else:
        body, scratch = _attn_fixed_kernel, []
    return pl.pallas_call(
        body,
        grid=(batch, N_KV_HEADS, seq // TQ),
        in_specs=[
            pl.BlockSpec((1, GQA_GROUP, QK_ROWS, TQ), lambda b, h, i: (b, h, 0, i)),
            pl.BlockSpec((1, 1, seq, QK_ROWS), lambda b, h, i: (b, h, 0, 0)),
            pl.BlockSpec((1, 1, v_rows, seq), lambda b, h, i: (b, h, 0, 0)),
        ],
        out_specs=pl.BlockSpec((1, grp, TQ), lambda b, h, i: (b, h, i)),
        out_shape=jax.ShapeDtypeStruct((batch, Q_DIM, seq), BF16),
        scratch_shapes=scratch,
        compiler_params=_params("parallel", "parallel", "parallel"),
        name="axial_attention_online" if online else "axial_attention",
    )(qT, k, vT)


def _memkv_kernel(mem_ref, g_ref, w_ref, kT_ref, v_ref):
    h = _rms(mem_ref[0], g_ref[0])
    kv = jnp.dot(h.astype(BF16), w_ref[0], preferred_element_type=F32)
    kT_ref[0, 0] = kv[:, :D_MODEL].T.astype(BF16)
    v_ref[0, 0] = kv[:, D_MODEL:].astype(BF16)


def _memkv_call(mem, gain, w, depth, batch):
    return pl.pallas_call(
        _memkv_kernel,
        grid=(depth, batch),
        in_specs=[
            pl.BlockSpec((1, MEM_LEN, D_MODEL), lambda l, b: (b, 0, 0)),
            pl.BlockSpec((1, 1, D_MODEL), lambda l, b: (l, 0, 0)),
            pl.BlockSpec((1, D_MODEL, 2 * D_MODEL), lambda l, b: (l, 0, 0)),
        ],
        out_specs=[
            pl.BlockSpec((1, 1, D_MODEL, MEM_LEN), lambda l, b: (l, b, 0, 0)),
            pl.BlockSpec((1, 1, MEM_LEN, D_MODEL), lambda l, b: (l, b, 0, 0)),
        ],
        out_shape=[
            jax.ShapeDtypeStruct((depth, batch, D_MODEL, MEM_LEN), BF16),
            jax.ShapeDtypeStruct((depth, batch, MEM_LEN, D_MODEL), BF16),
        ],
        compiler_params=_params("parallel", "parallel"),
        name="memory_kv",
    )(mem, gain, w)


def _xattn(x1, gx_ref, wq_ref, kT_ref, v_ref, wo_ref):
    h = _rms(x1, gx_ref[...])
    q = jnp.dot(h.astype(BF16), wq_ref[...], preferred_element_type=F32)
    q = q * ((X_HEAD_DIM ** -0.5) * LOG2E)
    outs = []
    for hd in range(X_HEADS):
        sl = slice(hd * X_HEAD_DIM, (hd + 1) * X_HEAD_DIM)
        s = jnp.dot(q[:, sl].astype(BF16), kT_ref[0, 0, sl, :],
                    preferred_element_type=F32)
        p = jnp.exp2(s - jnp.max(s, axis=-1, keepdims=True))
        l = jnp.sum(p, axis=-1, keepdims=True)
        o = jnp.dot(p.astype(BF16), v_ref[0, 0, :, sl], preferred_element_type=F32)
        outs.append((o / l).astype(BF16))
    o = jnp.concatenate(outs, axis=1)
    return x1 + jnp.dot(o, wo_ref[...], preferred_element_type=F32)


def _attn_out_xattn_kernel(x_ref, oT_ref, wo_ref, gx_ref, wq_ref, kT_ref, v_ref, wo2_ref,
                           out_ref):
    o = oT_ref[0].T
    x1 = x_ref[...] + jnp.dot(o, wo_ref[...], preferred_element_type=F32)
    out_ref[...] = _xattn(x1, gx_ref, wq_ref, kT_ref, v_ref, wo2_ref)


def _xattn_specs(layer, nt):
    return [
        _const_spec((1, D_MODEL)),
        _const_spec((D_MODEL, D_MODEL)),
        pl.BlockSpec((1, 1, D_MODEL, MEM_LEN), lambda i: (layer, i // nt, 0, 0)),
        pl.BlockSpec((1, 1, MEM_LEN, D_MODEL), lambda i: (layer, i // nt, 0, 0)),
        _const_spec((D_MODEL, D_MODEL)),
    ]


def _attn_out_xattn_call(x, oT, wo, gx, wq, kT, v, wo2, layer, batch, seq):
    nt = seq // TM
    return pl.pallas_call(
        _attn_out_xattn_kernel,
        grid=(batch * nt,),
        in_specs=[
            pl.BlockSpec((TM, D_MODEL), lambda i: (i, 0)),
            pl.BlockSpec((1, Q_DIM, TM), lambda i: (i // nt, 0, i % nt)),
            _const_spec((Q_DIM, D_MODEL)),
        ] + _xattn_specs(layer, nt),
        out_specs=pl.BlockSpec((TM, D_MODEL), lambda i: (i, 0)),
        out_shape=jax.ShapeDtypeStruct(x.shape, F32),
        compiler_params=_params("parallel"),
        name="attn_out_xattn",
    )(x, oT, wo, gx, wq, kT, v, wo2)


def _normed_with_halo(xp_ref, x_ref, xn_ref, gain, nt):
    i = pl.program_id(0)
    keep_prev = (i % nt != 0).astype(F32)
    keep_next = (i % nt != nt - 1).astype(F32)
    hp = _rms(xp_ref[...], gain) * keep_prev
    hn = _rms(xn_ref[...], gain) * keep_next
    return jnp.concatenate([hp, _rms(x_ref[...], gain), hn], axis=0)


def _halo_in_specs(n_tiles, tm):
    blocks = tm // HALO
    n_halo_blocks = n_tiles * blocks
    return [
        pl.BlockSpec((HALO, D_MODEL), lambda i: (jnp.maximum(i * blocks - 1, 0), 0)),
        pl.BlockSpec((tm, D_MODEL), lambda i: (i, 0)),
        pl.BlockSpec((HALO, D_MODEL),
                     lambda i: (jnp.minimum((i + 1) * blocks, n_halo_blocks - 1), 0)),
    ]


def _pool_xattn_kernel(xp_ref, x_ref, xn_ref, gp_ref, wp_ref, sc_ref,
                       gx_ref, wq_ref, kT_ref, v_ref, wo2_ref, out_ref, *, nt, seq):
    i = pl.program_id(0)
    rows = TM + 2 * HALO
    h = _normed_with_halo(xp_ref, x_ref, xn_ref, gp_ref[...], nt)
    pos = (i % nt) * TM + lax.broadcasted_iota(jnp.int32, (TM, 1), 0)
    ys = []
    for g, w in enumerate(POOL_WINDOWS):
        sl = slice(g * POOL_GROUP_W, (g + 1) * POOL_GROUP_W)
        hg = h[:, sl]
        acc = hg
        span = 1
        while span < w:
            acc = acc + pltpu.roll(acc, span, 0)
            span *= 2
        shift = w // 2 - 1
        if shift:
            acc = pltpu.roll(acc, rows - shift, 0)
        cnt = (jnp.minimum(pos + w // 2, seq) - jnp.maximum(pos - w // 2, 0)).astype(F32)
        mixed = acc[HALO:HALO + TM] * (1.0 / cnt) - hg[HALO:HALO + TM]
        ys.append(jnp.dot(mixed.astype(BF16), wp_ref[g], preferred_element_type=F32))
    y = jnp.concatenate(ys, axis=1) * sc_ref[...]
    x1 = x_ref[...] + y
    out_ref[...] = _xattn(x1, gx_ref, wq_ref, kT_ref, v_ref, wo2_ref)


def _pool_xattn_call(x, gp, wp, sc, gx, wq, kT, v, wo2, layer, batch, seq):
    nt = seq // TM
    return pl.pallas_call(
        functools.partial(_pool_xattn_kernel, nt=nt, seq=seq),
        grid=(batch * nt,),
        in_specs=_halo_in_specs(batch * nt, TM) + [
            _const_spec((1, D_MODEL)),
            _const_spec((len(POOL_WINDOWS), POOL_GROUP_W, POOL_GROUP_W)),
            _const_spec((1, D_MODEL)),
        ] + _xattn_specs(layer, nt),
        out_specs=pl.BlockSpec((TM, D_MODEL), lambda i: (i, 0)),
        out_shape=jax.ShapeDtypeStruct(x.shape, F32),
        compiler_params=_params("parallel"),
        name="pool_xattn",
    )(x, x, x, gp, wp, sc, gx, wq, kT, v, wo2)


def _ffn_kernel(xp_ref, x_ref, xn_ref, g_ref, wup_ref, cw_ref, cb_ref, wdn_ref, gf_ref,
                out_ref, h_ref, act_ref, *, nt, final_norm):
    tm = x_ref.shape[0]
    rows = tm + 2 * HALO
    h_ref[...] = _normed_with_halo(xp_ref, x_ref, xn_ref, g_ref[...], nt).astype(BF16)

    def conv_up(off):
        u = jnp.dot(h_ref[...], wup_ref[:, off:off + FF_CHUNK], preferred_element_type=F32)
        cw = cw_ref[:, off:off + FF_CHUNK]
        c = (pltpu.roll(u, 1, 0) * cw[0:1] + u * cw[1:2]
             + pltpu.roll(u, rows - 1, 0) * cw[2:3] + cb_ref[:, off:off + FF_CHUNK])
        return c[HALO:HALO + tm]

    for c in range(D_FF // FF_CHUNK):
        gate = conv_up(c * FF_CHUNK)
        val = conv_up(D_FF + c * FF_CHUNK)
        act = gate * (1.0 / (1.0 + jnp.exp(-gate))) * val
        act_ref[:, c * FF_CHUNK:(c + 1) * FF_CHUNK] = act.astype(BF16)
    y = x_ref[...] + jnp.dot(act_ref[...], wdn_ref[...], preferred_element_type=F32)
    if final_norm:
        y = _rms(y, gf_ref[...])
    out_ref[...] = y


def _ffn_call(x, gain, wup, cw, cb, wdn, gf, final_norm, batch, seq):
    nt = seq // TM_FFN
    return pl.pallas_call(
        functools.partial(_ffn_kernel, nt=nt, final_norm=final_norm),
        grid=(batch * nt,),
        in_specs=_halo_in_specs(batch * nt, TM_FFN) + [
            _const_spec((1, D_MODEL)),
            _const_spec((D_MODEL, 2 * D_FF)),
            _const_spec((3, 2 * D_FF)),
            _const_spec((1, 2 * D_FF)),
            _const_spec((D_FF, D_MODEL)),
            _const_spec((1, D_MODEL)),
        ],
        out_specs=pl.BlockSpec((TM_FFN, D_MODEL), lambda i: (i, 0)),
        out_shape=jax.ShapeDtypeStruct(x.shape, F32),
        scratch_shapes=[
            pltpu.VMEM((TM_FFN + 2 * HALO, D_MODEL), BF16),
            pltpu.VMEM((TM_FFN, D_FF), BF16),
        ],
        compiler_params=_params("parallel"),
        name="conv_ffn",
    )(x, x, x, gain, wup, cw, cb, wdn, gf)


def _rope_tables(seq):
    t = jnp.arange(seq, dtype=jnp.int32)
    row = (t // GRID_W).astype(F32)
    col = (t % GRID_W).astype(F32)
    inv_freq = ROPE_THETA ** (-jnp.arange(ROPE_PAIRS, dtype=F32) / ROPE_PAIRS)
    ang_row = row[:, None] * inv_freq
    ang_col = col[:, None] * inv_freq
    ang = jnp.concatenate([ang_row, ang_row, ang_col, ang_col], axis=1)
    sign = jnp.concatenate([-jnp.ones((ROPE_PAIRS,), F32), jnp.ones((ROPE_PAIRS,), F32)] * 2)
    cos = jnp.tile(jnp.cos(ang), (1, 2))
    sin = jnp.tile(jnp.sin(ang) * sign, (1, 2))
    return cos, sin


def _head_indicators():
    col_head = jnp.arange(QK_DIM) // HEAD_DIM
    e = (col_head[:, None] == jnp.arange(LANES)[None, :]).astype(F32)
    e1 = (e / HEAD_DIM).astype(BF16)
    e2 = jnp.concatenate([e.T, e.T], axis=0).astype(BF16)
    return e1, e2


def kernel(x, mem, attn_norm, attn_w_qkv, attn_q_gain, attn_k_gain, attn_w_o, pool_norm, pool_w, pool_scale, xattn_norm, mem_norm, xattn_w_q, xattn_w_kv, xattn_w_o, ffn_norm, ffn_w_up, ffn_conv_w, ffn_conv_b, ffn_w_down, final_norm):
    batch, seq, d = x.shape
    depth = xattn_norm.shape[0]
    assert d == D_MODEL and seq % TM == 0 and seq % TM_FFN == 0 and seq % TQ == 0 and seq % TK == 0
    xf = x.reshape(batch * seq, d)

    cos, sin = _rope_tables(seq)
    e1, e2 = _head_indicators()
    kT_mem, v_mem = _memkv_call(mem, mem_norm.reshape(depth, 1, d),
                                xattn_w_kv.astype(BF16), depth, batch)

    ia = ib = 0
    for i in range(depth):
        xa = (xattn_norm[i].reshape(1, d), xattn_w_q[i].astype(BF16), kT_mem, v_mem,
              xattn_w_o[i].astype(BF16))
        if i % 2 == 0:
            bound = (HEAD_DIM ** 0.5) * LOG2E * jnp.max(jnp.abs(attn_q_gain[ia])) * jnp.max(
                jnp.abs(attn_k_gain[ia]))
            fixed_ok = bound <= MAX_STATIC_SHIFT
            shift = jnp.where(fixed_ok, bound, 0.0)
            qx = jnp.zeros((QK_ROWS - HEAD_DIM, TM), F32).at[0].set(-shift).astype(BF16)
            kx = jnp.zeros((TM, QK_ROWS - HEAD_DIM), BF16).at[:, 0].set(1.0)
            qT, k, vT = _qkv_call(
                xf, attn_norm[ia].reshape(1, d), attn_w_qkv[ia].astype(BF16),
                jnp.tile(attn_q_gain[ia], 2).reshape(1, LANES),
                jnp.tile(attn_k_gain[ia], 2).reshape(1, LANES),
                cos, sin, e1, e2, qx, kx, batch, seq)
            oT = lax.cond(
                fixed_ok,
                functools.partial(_attn_call, batch=batch, seq=seq, online=False),
                functools.partial(_attn_call, batch=batch, seq=seq, online=True),
                qT, k, vT)
            xf = _attn_out_xattn_call(xf, oT, attn_w_o[ia].astype(BF16), *xa, i, batch, seq)
            ia += 1
        else:
            xf = _pool_xattn_call(xf, pool_norm[ib].reshape(1, d), pool_w[ib].astype(BF16),
                                  pool_scale[ib].reshape(1, d), *xa, i, batch, seq)
            ib += 1
        xf = _ffn_call(xf, ffn_norm[i].reshape(1, d), ffn_w_up[i].astype(BF16),
                       ffn_conv_w[i], ffn_conv_b[i].reshape(1, 2 * D_FF),
                       ffn_w_down[i].astype(BF16), final_norm.reshape(1, d),
                       i == depth - 1, batch, seq)
    return xf.reshape(batch, seq, d)
```

```python
import functools
import math

import jax
import jax.numpy as jnp
from jax import lax
from jax.experimental import pallas as pl
from jax.experimental.pallas import tpu as pltpu

F32 = jnp.float32
BF16 = jnp.bfloat16

D_MODEL = 1024
GRID_W = 64
EPS = 1e-6
HEAD_DIM = 64
N_HEADS = 16
N_KV_HEADS = 4
GQA_GROUP = 4
ROPE_THETA = 10000.0
ROPE_PAIRS = 16
Q_DIM = N_HEADS * HEAD_DIM
KV_DIM = N_KV_HEADS * HEAD_DIM
QK_DIM = Q_DIM + KV_DIM
QKV_DIM = Q_DIM + 2 * KV_DIM
POOL_WINDOWS = (2, 4, 8, 16)
POOL_GROUP_W = 256
MEM_LEN = 256
X_HEADS = 4
X_HEAD_DIM = 256
D_FF = 2816
LOG2E = math.log2(math.e)

HALO = 16
SUM_ROWS = 16
VMEM_LIMIT = 56 * 1024 * 1024

TM = 1024
TM_FFN = 1024
TQ = 512
TK = 256
QK_ROWS = 2 * HEAD_DIM
MAX_STATIC_SHIFT = 40.0
FF_CHUNK = 256
POOL_ROWS = 256


def _rms(x, gain):
    return x * lax.rsqrt(jnp.mean(x * x, axis=-1, keepdims=True) + EPS) * gain


def _params(*sem):
    return pltpu.CompilerParams(dimension_semantics=sem, vmem_limit_bytes=VMEM_LIMIT)


def _const_spec(shape):
    nd = len(shape)
    return pl.BlockSpec(shape, lambda *_: (0,) * nd)


def _rope_rows(y):
    p = ROPE_PAIRS
    return jnp.concatenate([y[p:2 * p], y[:p], y[3 * p:], y[2 * p:3 * p]], axis=0)


def _qkv_kernel(x_ref, g_ref, wT_ref, cq_ref, sq_ref, ck_ref, sk_ref, qx_ref, kx_ref,
                qT_ref, k_ref, vT_ref):
    h = _rms(x_ref[...], g_ref[...])
    yT = jnp.dot(wT_ref[...], h.T.astype(BF16), preferred_element_type=F32)

    def norm_rope(hd, cos, sin):
        y = yT[hd * HEAD_DIM:(hd + 1) * HEAD_DIM]
        r = lax.rsqrt(jnp.mean(y * y, axis=0, keepdims=True) + EPS)
        return (y * cos + _rope_rows(y) * sin) * r

    cq, sq = cq_ref[...], sq_ref[...]
    for hd in range(N_HEADS):
        qT_ref[0, hd, :HEAD_DIM, :] = norm_rope(hd, cq, sq).astype(BF16)
        qT_ref[0, hd, HEAD_DIM:, :] = qx_ref[...]
    ck, sk = ck_ref[...], sk_ref[...]
    kT = jnp.concatenate([norm_rope(N_HEADS + g, ck, sk) for g in range(N_KV_HEADS)], axis=0)
    k = kT.T.astype(BF16)
    ones = jnp.ones((SUM_ROWS, yT.shape[1]), BF16)
    for g in range(N_KV_HEADS):
        sl = slice(g * HEAD_DIM, (g + 1) * HEAD_DIM)
        k_ref[0, g] = jnp.concatenate([k[:, sl], kx_ref[...]], axis=1)
        vT = yT[QK_DIM + g * HEAD_DIM:QK_DIM + (g + 1) * HEAD_DIM].astype(BF16)
        vT_ref[0, g] = jnp.concatenate([vT, ones], axis=0)


def _qkv_call(x, gain, wT, cq, sq, ck, sk, qx, kx, batch, seq):
    nt = seq // TM
    v_rows = HEAD_DIM + SUM_ROWS
    table = pl.BlockSpec((HEAD_DIM, TM), lambda i: (0, i % nt))
    return pl.pallas_call(
        _qkv_kernel,
        grid=(batch * nt,),
        in_specs=[
            pl.BlockSpec((TM, D_MODEL), lambda i: (i, 0)),
            _const_spec((1, D_MODEL)),
            _const_spec((QKV_DIM, D_MODEL)),
            table, table, table, table,
            _const_spec((QK_ROWS - HEAD_DIM, TM)),
            _const_spec((TM, QK_ROWS - HEAD_DIM)),
        ],
        out_specs=[
            pl.BlockSpec((1, N_HEADS, QK_ROWS, TM), lambda i: (i // nt, 0, 0, i % nt)),
            pl.BlockSpec((1, N_KV_HEADS, TM, QK_ROWS), lambda i: (i // nt, 0, i % nt, 0)),
            pl.BlockSpec((1, N_KV_HEADS, v_rows, TM), lambda i: (i // nt, 0, 0, i % nt)),
        ],
        out_shape=[
            jax.ShapeDtypeStruct((batch, N_HEADS, QK_ROWS, seq), BF16),
            jax.ShapeDtypeStruct((batch, N_KV_HEADS, seq, QK_ROWS), BF16),
            jax.ShapeDtypeStruct((batch, N_KV_HEADS, v_rows, seq), BF16),
        ],
        compiler_params=_params("parallel"),
        name="qkv_rope",
    )(x, gain, wT, cq, sq, ck, sk, qx, kx)


def _attn_fixed_kernel(qT_ref, k_ref, vT_ref, oT_ref):
    q = jnp.concatenate([qT_ref[0, g] for g in range(GQA_GROUP)], axis=1)
    s = jnp.dot(k_ref[0, 0], q, preferred_element_type=F32)
    p = jnp.exp2(s).astype(BF16)
    o = jnp.dot(vT_ref[0, 0], p, preferred_element_type=F32)
    o = o[:HEAD_DIM] / o[HEAD_DIM:HEAD_DIM + 1]
    for g in range(GQA_GROUP):
        oT_ref[0, g * HEAD_DIM:(g + 1) * HEAD_DIM, :] = o[:, g * TQ:(g + 1) * TQ].astype(BF16)


def _attn_online_kernel(qT_ref, k_ref, vT_ref, oT_ref, s0_ref, s1_ref, m_ref, acc_ref, *, seq):
    n_chunks = seq // TK

    def scores(c, s_ref):
        q = jnp.concatenate([qT_ref[0, g] for g in range(GQA_GROUP)], axis=1)
        kc = k_ref[0, 0, pl.ds(pl.multiple_of(c * TK, TK), TK), :]
        s_ref[...] = jnp.dot(kc, q, preferred_element_type=F32)

    def softmax_pv(c, s_ref):
        s = s_ref[...]
        m_prev = m_ref[...]
        m_new = jnp.maximum(m_prev, jnp.max(s, axis=0, keepdims=True))
        alpha = jnp.exp2(m_prev - m_new)
        p = jnp.exp2(s - m_new).astype(BF16)
        vc = vT_ref[0, 0, :, pl.ds(pl.multiple_of(c * TK, TK), TK)]
        acc_ref[...] = alpha * acc_ref[...] + jnp.dot(vc, p, preferred_element_type=F32)
        m_ref[...] = m_new

    m_ref[...] = jnp.full(m_ref.shape, -1e30, F32)
    acc_ref[...] = jnp.zeros(acc_ref.shape, F32)
    scores(0, s0_ref)

    def body(i, carry):
        c = 2 * i
        scores(c + 1, s1_ref)
        softmax_pv(c, s0_ref)
        scores(c + 2, s0_ref)
        softmax_pv(c + 1, s1_ref)
        return carry

    lax.fori_loop(0, n_chunks // 2 - 1, body, 0)
    scores(n_chunks - 1, s1_ref)
    softmax_pv(n_chunks - 2, s0_ref)
    softmax_pv(n_chunks - 1, s1_ref)
    acc = acc_ref[...]
    o = acc[:HEAD_DIM] / acc[HEAD_DIM:HEAD_DIM + 1]
    for g in range(GQA_GROUP):
        oT_ref[0, g * HEAD_DIM:(g + 1) * HEAD_DIM, :] = o[:, g * TQ:(g + 1) * TQ].astype(BF16)


def _attn_call(qT, k, vT, batch, seq, online):
    grp = GQA_GROUP * HEAD_DIM
    lanes = GQA_GROUP * TQ
    v_rows = HEAD_DIM + SUM_ROWS
    if online:
        body = functools.partial(_attn_online_kernel, seq=seq)
        scratch = [
            pltpu.VMEM((TK, lanes), F32),
            pltpu.VMEM((TK, lanes), F32),
            pltpu.VMEM((1, lanes), F32),
            pltpu.VMEM((v_rows, lanes), F32),
        ]
    else:
        body, scratch = _attn_fixed_kernel, []
    return pl.pallas_call(
        body,
        grid=(batch, N_KV_HEADS, seq // TQ),
        in_specs=[
            pl.BlockSpec((1, GQA_GROUP, QK_ROWS, TQ), lambda b, h, i: (b, h, 0, i)),
            pl.BlockSpec((1, 1, seq, QK_ROWS), lambda b, h, i: (b, h, 0, 0)),
            pl.BlockSpec((1, 1, v_rows, seq), lambda b, h, i: (b, h, 0, 0)),
        ],
        out_specs=pl.BlockSpec((1, grp, TQ), lambda b, h, i: (b, h, i)),
        out_shape=jax.ShapeDtypeStruct((batch, Q_DIM, seq), BF16),
        scratch_shapes=scratch,
        compiler_params=_params("parallel", "parallel", "parallel"),
        name="axial_attention_online" if online else "axial_attention",
    )(qT, k, vT)


def _memkv_kernel(mem_ref, g_ref, w_ref, kT_ref, v_ref):
    h = _rms(mem_ref[0], g_ref[0])
    kv = jnp.dot(h.astype(BF16), w_ref[0], preferred_element_type=F32)
    kT_ref[0, 0] = kv[:, :D_MODEL].T.astype(BF16)
    v_ref[0, 0] = kv[:, D_MODEL:].astype(BF16)


def _memkv_call(mem, gain, w, depth, batch):
    return pl.pallas_call(
        _memkv_kernel,
        grid=(depth, batch),
        in_specs=[
            pl.BlockSpec((1, MEM_LEN, D_MODEL), lambda l, b: (b, 0, 0)),
            pl.BlockSpec((1, 1, D_MODEL), lambda l, b: (l, 0, 0)),
            pl.BlockSpec((1, D_MODEL, 2 * D_MODEL), lambda l, b: (l, 0, 0)),
        ],
        out_specs=[
            pl.BlockSpec((1, 1, D_MODEL, MEM_LEN), lambda l, b: (l, b, 0, 0)),
            pl.BlockSpec((1, 1, MEM_LEN, D_MODEL), lambda l, b: (l, b, 0, 0)),
        ],
        out_shape=[
            jax.ShapeDtypeStruct((depth, batch, D_MODEL, MEM_LEN), BF16),
            jax.ShapeDtypeStruct((depth, batch, MEM_LEN, D_MODEL), BF16),
        ],
        compiler_params=_params("parallel", "parallel"),
        name="memory_kv",
    )(mem, gain, w)


def _xattn(x1, gx_ref, wq_ref, kT_ref, v_ref, wo_ref):
    h = _rms(x1, gx_ref[...])
    q = jnp.dot(h.astype(BF16), wq_ref[...], preferred_element_type=F32)
    q = q * ((X_HEAD_DIM ** -0.5) * LOG2E)
    outs = []
    for hd in range(X_HEADS):
        sl = slice(hd * X_HEAD_DIM, (hd + 1) * X_HEAD_DIM)
        s = jnp.dot(q[:, sl].astype(BF16), kT_ref[0, 0, sl, :],
                    preferred_element_type=F32)
        p = jnp.exp2(s - jnp.max(s, axis=-1, keepdims=True))
        l = jnp.sum(p, axis=-1, keepdims=True)
        o = jnp.dot(p.astype(BF16), v_ref[0, 0, :, sl], preferred_element_type=F32)
        outs.append((o / l).astype(BF16))
    o = jnp.concatenate(outs, axis=1)
    return x1 + jnp.dot(o, wo_ref[...], preferred_element_type=F32)


def _attn_out_xattn_kernel(x_ref, oT_ref, wo_ref, gx_ref, wq_ref, kT_ref, v_ref, wo2_ref,
                           out_ref):
    o = oT_ref[0].T
    x1 = x_ref[...] + jnp.dot(o, wo_ref[...], preferred_element_type=F32)
    out_ref[...] = _xattn(x1, gx_ref, wq_ref, kT_ref, v_ref, wo2_ref)


def _xattn_specs(layer, nt):
    return [
        _const_spec((1, D_MODEL)),
        _const_spec((D_MODEL, D_MODEL)),
        pl.BlockSpec((1, 1, D_MODEL, MEM_LEN), lambda i: (layer, i // nt, 0, 0)),
        pl.BlockSpec((1, 1, MEM_LEN, D_MODEL), lambda i: (layer, i // nt, 0, 0)),
        _const_spec((D_MODEL, D_MODEL)),
    ]


def _attn_out_xattn_call(x, oT, wo, gx, wq, kT, v, wo2, layer, batch, seq):
    nt = seq // TM
    return pl.pallas_call(
        _attn_out_xattn_kernel,
        grid=(batch * nt,),
        in_specs=[
            pl.BlockSpec((TM, D_MODEL), lambda i: (i, 0)),
            pl.BlockSpec((1, Q_DIM, TM), lambda i: (i // nt, 0, i % nt)),
            _const_spec((Q_DIM, D_MODEL)),
        ] + _xattn_specs(layer, nt),
        out_specs=pl.BlockSpec((TM, D_MODEL), lambda i: (i, 0)),
        out_shape=jax.ShapeDtypeStruct(x.shape, F32),
        compiler_params=_params("parallel"),
        name="attn_out_xattn",
    )(x, oT, wo, gx, wq, kT, v, wo2)


def _normed_with_halo(xp_ref, x_ref, xn_ref, gain, nt):
    i = pl.program_id(0)
    keep_prev = (i % nt != 0).astype(F32)
    keep_next = (i % nt != nt - 1).astype(F32)
    hp = _rms(xp_ref[...], gain) * keep_prev
    hn = _rms(xn_ref[...], gain) * keep_next
    return jnp.concatenate([hp, _rms(x_ref[...], gain), hn], axis=0)


def _halo_in_specs(n_tiles, tm):
    blocks = tm // HALO
    n_halo_blocks = n_tiles * blocks
    return [
        pl.BlockSpec((HALO, D_MODEL), lambda i: (jnp.maximum(i * blocks - 1, 0), 0)),
        pl.BlockSpec((tm, D_MODEL), lambda i: (i, 0)),
        pl.BlockSpec((HALO, D_MODEL),
                     lambda i: (jnp.minimum((i + 1) * blocks, n_halo_blocks - 1), 0)),
    ]


def _pool_xattn_kernel(xp_ref, x_ref, xn_ref, gp_ref, band_ref, wp_ref, sc_ref,
                       gx_ref, wq_ref, kT_ref, v_ref, wo2_ref, out_ref, *, nt, seq):
    i = pl.program_id(0)
    h = _normed_with_halo(xp_ref, x_ref, xn_ref, gp_ref[...], nt)
    hb = h.astype(BF16)
    pos = (i % nt) * TM + lax.broadcasted_iota(jnp.int32, (TM, 1), 0)
    ys = []
    for g, w in enumerate(POOL_WINDOWS):
        sl = slice(g * POOL_GROUP_W, (g + 1) * POOL_GROUP_W)
        cnt = (jnp.minimum(pos + w // 2, seq) - jnp.maximum(pos - w // 2, 0)).astype(F32)
        sums = jnp.concatenate(
            [jnp.dot(band_ref[g], hb[r:r + POOL_ROWS + 2 * HALO, sl], preferred_element_type=F32)
             for r in range(0, TM, POOL_ROWS)], axis=0)
        mixed = sums * (1.0 / cnt) - h[HALO:HALO + TM, sl]
        ys.append(jnp.dot(mixed.astype(BF16), wp_ref[g], preferred_element_type=F32))
    y = jnp.concatenate(ys, axis=1) * sc_ref[...]
    x1 = x_ref[...] + y
    out_ref[...] = _xattn(x1, gx_ref, wq_ref, kT_ref, v_ref, wo2_ref)


def _pool_bands():
    i = jnp.arange(POOL_ROWS)[:, None]
    m = jnp.arange(POOL_ROWS + 2 * HALO)[None, :] - HALO
    return jnp.stack([((m >= i - w // 2) & (m < i + w // 2)).astype(BF16) for w in POOL_WINDOWS])


def _pool_xattn_call(x, gp, wp, sc, gx, wq, kT, v, wo2, layer, batch, seq):
    nt = seq // TM
    assert max(POOL_WINDOWS) // 2 <= HALO and TM % POOL_ROWS == 0
    return pl.pallas_call(
        functools.partial(_pool_xattn_kernel, nt=nt, seq=seq),
        grid=(batch * nt,),
        in_specs=_halo_in_specs(batch * nt, TM) + [
            _const_spec((1, D_MODEL)),
            _const_spec((len(POOL_WINDOWS), POOL_ROWS, POOL_ROWS + 2 * HALO)),
            _const_spec((len(POOL_WINDOWS), POOL_GROUP_W, POOL_GROUP_W)),
            _const_spec((1, D_MODEL)),
        ] + _xattn_specs(layer, nt),
        out_specs=pl.BlockSpec((TM, D_MODEL), lambda i: (i, 0)),
        out_shape=jax.ShapeDtypeStruct(x.shape, F32),
        compiler_params=_params("parallel"),
        name="pool_xattn",
    )(x, x, x, gp, _pool_bands(), wp, sc, gx, wq, kT, v, wo2)


def _ffn_kernel(xp_ref, x_ref, xn_ref, g_ref, wup_ref, cw_ref, cb_ref, wdn_ref, gf_ref,
                out_ref, h_ref, act_ref, *, nt, final_norm):
    tm = x_ref.shape[0]
    rows = tm + 2 * HALO
    h_ref[...] = _normed_with_halo(xp_ref, x_ref, xn_ref, g_ref[...], nt).astype(BF16)

    def conv_up(off):
        u = jnp.dot(h_ref[...], wup_ref[:, off:off + FF_CHUNK], preferred_element_type=F32)
        cw = cw_ref[:, off:off + FF_CHUNK]
        c = (pltpu.roll(u, 1, 0) * cw[0:1] + u * cw[1:2]
             + pltpu.roll(u, rows - 1, 0) * cw[2:3] + cb_ref[:, off:off + FF_CHUNK])
        return c[HALO:HALO + tm]

    for c in range(D_FF // FF_CHUNK):
        gate = conv_up(c * FF_CHUNK)
        val = conv_up(D_FF + c * FF_CHUNK)
        act = gate * (1.0 / (1.0 + jnp.exp(-gate))) * val
        act_ref[:, c * FF_CHUNK:(c + 1) * FF_CHUNK] = act.astype(BF16)
    y = x_ref[...] + jnp.dot(act_ref[...], wdn_ref[...], preferred_element_type=F32)
    if final_norm:
        y = _rms(y, gf_ref[...])
    out_ref[...] = y


def _ffn_call(x, gain, wup, cw, cb, wdn, gf, final_norm, batch, seq):
    nt = seq // TM_FFN
    return pl.pallas_call(
        functools.partial(_ffn_kernel, nt=nt, final_norm=final_norm),
        grid=(batch * nt,),
        in_specs=_halo_in_specs(batch * nt, TM_FFN) + [
            _const_spec((1, D_MODEL)),
            _const_spec((D_MODEL, 2 * D_FF)),
            _const_spec((3, 2 * D_FF)),
            _const_spec((1, 2 * D_FF)),
            _const_spec((D_FF, D_MODEL)),
            _const_spec((1, D_MODEL)),
        ],
        out_specs=pl.BlockSpec((TM_FFN, D_MODEL), lambda i: (i, 0)),
        out_shape=jax.ShapeDtypeStruct(x.shape, F32),
        scratch_shapes=[
            pltpu.VMEM((TM_FFN + 2 * HALO, D_MODEL), BF16),
            pltpu.VMEM((TM_FFN, D_FF), BF16),
        ],
        compiler_params=_params("parallel"),
        name="conv_ffn",
    )(x, x, x, gain, wup, cw, cb, wdn, gf)


def _rope_tables(seq, gain, scale):
    t = jnp.arange(seq, dtype=jnp.int32)
    row = (t // GRID_W).astype(F32)
    col = (t % GRID_W).astype(F32)
    inv_freq = ROPE_THETA ** (-jnp.arange(ROPE_PAIRS, dtype=F32) / ROPE_PAIRS)
    ang_row = inv_freq[:, None] * row[None, :]
    ang_col = inv_freq[:, None] * col[None, :]
    ang = jnp.concatenate([ang_row, ang_row, ang_col, ang_col], axis=0)
    sign = jnp.concatenate([-jnp.ones((ROPE_PAIRS,), F32), jnp.ones((ROPE_PAIRS,), F32)] * 2)
    g = gain.astype(F32) * scale
    g_partner = _rope_rows(g[:, None])[:, 0]
    return jnp.cos(ang) * g[:, None], jnp.sin(ang) * (sign * g_partner)[:, None]


def kernel(x, mem, attn_norm, attn_w_qkv, attn_q_gain, attn_k_gain, attn_w_o, pool_norm, pool_w, pool_scale, xattn_norm, mem_norm, xattn_w_q, xattn_w_kv, xattn_w_o, ffn_norm, ffn_w_up, ffn_conv_w, ffn_conv_b, ffn_w_down, final_norm):
    batch, seq, d = x.shape
    depth = xattn_norm.shape[0]
    assert d == D_MODEL and seq % TM == 0 and seq % TM_FFN == 0 and seq % TQ == 0 and seq % TK == 0
    xf = x.reshape(batch * seq, d)

    kT_mem, v_mem = _memkv_call(mem, mem_norm.reshape(depth, 1, d),
                                xattn_w_kv.astype(BF16), depth, batch)

    ia = ib = 0
    for i in range(depth):
        xa = (xattn_norm[i].reshape(1, d), xattn_w_q[i].astype(BF16), kT_mem, v_mem,
              xattn_w_o[i].astype(BF16))
        if i % 2 == 0:
            bound = (HEAD_DIM ** 0.5) * LOG2E * jnp.max(jnp.abs(attn_q_gain[ia])) * jnp.max(
                jnp.abs(attn_k_gain[ia]))
            fixed_ok = bound <= MAX_STATIC_SHIFT
            shift = jnp.where(fixed_ok, bound, 0.0)
            qx = jnp.zeros((QK_ROWS - HEAD_DIM, TM), F32).at[0].set(-shift).astype(BF16)
            kx = jnp.zeros((TM, QK_ROWS - HEAD_DIM), BF16).at[:, 0].set(1.0)
            cq, sq = _rope_tables(seq, attn_q_gain[ia], (HEAD_DIM ** -0.5) * LOG2E)
            ck, sk = _rope_tables(seq, attn_k_gain[ia], 1.0)
            qT, k, vT = _qkv_call(
                xf, attn_norm[ia].reshape(1, d), attn_w_qkv[ia].T.astype(BF16),
                cq, sq, ck, sk, qx, kx, batch, seq)
            oT = lax.cond(
                fixed_ok,
                functools.partial(_attn_call, batch=batch, seq=seq, online=False),
                functools.partial(_attn_call, batch=batch, seq=seq, online=True),
                qT, k, vT)
            xf = _attn_out_xattn_call(xf, oT, attn_w_o[ia].astype(BF16), *xa, i, batch, seq)
            ia += 1
        else:
            xf = _pool_xattn_call(xf, pool_norm[ib].reshape(1, d), pool_w[ib].astype(BF16),
                                  pool_scale[ib].reshape(1, d), *xa, i, batch, seq)
            ib += 1
        xf = _ffn_call(xf, ffn_norm[i].reshape(1, d), ffn_w_up[i].astype(BF16),
                       ffn_conv_w[i], ffn_conv_b[i].reshape(1, 2 * D_FF),
                       ffn_w_down[i].astype(BF16), final_norm.reshape(1, d),
                       i == depth - 1, batch, seq)
    return xf.reshape(batch, seq, d)
```

```python
import functools
import math

import jax
import jax.numpy as jnp
from jax import lax
from jax.experimental import pallas as pl
from jax.experimental.pallas import tpu as pltpu

F32 = jnp.float32
BF16 = jnp.bfloat16

D_MODEL = 1024
GRID_W = 64
EPS = 1e-6
HEAD_DIM = 64
N_HEADS = 16
N_KV_HEADS = 4
GQA_GROUP = 4
ROPE_THETA = 10000.0
ROPE_PAIRS = 16
Q_DIM = N_HEADS * HEAD_DIM
KV_DIM = N_KV_HEADS * HEAD_DIM
QK_DIM = Q_DIM + KV_DIM
QKV_DIM = Q_DIM + 2 * KV_DIM
POOL_WINDOWS = (2, 4, 8, 16)
POOL_GROUP_W = 256
MEM_LEN = 256
X_HEADS = 4
X_HEAD_DIM = 256
D_FF = 2816
LOG2E = math.log2(math.e)

HALO = 16
SUM_ROWS = 16
VMEM_LIMIT = 56 * 1024 * 1024

TM = 1024
TM_FFN = 1024
TQ = 512
TK = 256
QK_ROWS = 2 * HEAD_DIM
MAX_STATIC_SHIFT = 40.0
FF_CHUNK = 256
POOL_ROWS = 256
QKV_ROW_BLOCKS = 6
FFN_ROW_BLOCKS = 3


def _rms(x, gain):
    return x * lax.rsqrt(jnp.mean(x * x, axis=-1, keepdims=True) + EPS) * gain


def _params(*sem):
    return pltpu.CompilerParams(dimension_semantics=sem, vmem_limit_bytes=VMEM_LIMIT)


def _dot_rows(a, b, blocks):
    m = a.shape[0] // blocks
    return jnp.concatenate(
        [jnp.dot(a[r * m:(r + 1) * m], b, preferred_element_type=F32) for r in range(blocks)],
        axis=0)


def _const_spec(shape):
    nd = len(shape)
    return pl.BlockSpec(shape, lambda *_: (0,) * nd)


def _rope_rows(y):
    p = ROPE_PAIRS
    return jnp.concatenate([y[p:2 * p], y[:p], y[3 * p:], y[2 * p:3 * p]], axis=0)


def _qkv_kernel(x_ref, g_ref, wT_ref, cq_ref, sq_ref, ck_ref, sk_ref, qx_ref, kx_ref,
                qT_ref, k_ref, vT_ref):
    h = _rms(x_ref[...], g_ref[...])
    yT = _dot_rows(wT_ref[...], h.T.astype(BF16), QKV_ROW_BLOCKS)

    def norm_rope(hd, cos, sin):
        y = yT[hd * HEAD_DIM:(hd + 1) * HEAD_DIM]
        r = lax.rsqrt(jnp.mean(y * y, axis=0, keepdims=True) + EPS)
        return (y * cos + _rope_rows(y) * sin) * r

    cq, sq = cq_ref[...], sq_ref[...]
    for hd in range(N_HEADS):
        qT_ref[0, hd, :HEAD_DIM, :] = norm_rope(hd, cq, sq).astype(BF16)
        qT_ref[0, hd, HEAD_DIM:, :] = qx_ref[...]
    ck, sk = ck_ref[...], sk_ref[...]
    kT = jnp.concatenate([norm_rope(N_HEADS + g, ck, sk) for g in range(N_KV_HEADS)], axis=0)
    k = kT.T.astype(BF16)
    ones = jnp.ones((SUM_ROWS, yT.shape[1]), BF16)
    for g in range(N_KV_HEADS):
        sl = slice(g * HEAD_DIM, (g + 1) * HEAD_DIM)
        k_ref[0, g] = jnp.concatenate([k[:, sl], kx_ref[...]], axis=1)
        vT = yT[QK_DIM + g * HEAD_DIM:QK_DIM + (g + 1) * HEAD_DIM].astype(BF16)
        vT_ref[0, g] = jnp.concatenate([vT, ones], axis=0)


def _qkv_call(x, gain, wT, cq, sq, ck, sk, qx, kx, batch, seq):
    nt = seq // TM
    v_rows = HEAD_DIM + SUM_ROWS
    table = pl.BlockSpec((HEAD_DIM, TM), lambda i: (0, i % nt))
    return pl.pallas_call(
        _qkv_kernel,
        grid=(batch * nt,),
        in_specs=[
            pl.BlockSpec((TM, D_MODEL), lambda i: (i, 0)),
            _const_spec((1, D_MODEL)),
            _const_spec((QKV_DIM, D_MODEL)),
            table, table, table, table,
            _const_spec((QK_ROWS - HEAD_DIM, TM)),
            _const_spec((TM, QK_ROWS - HEAD_DIM)),
        ],
        out_specs=[
            pl.BlockSpec((1, N_HEADS, QK_ROWS, TM), lambda i: (i // nt, 0, 0, i % nt)),
            pl.BlockSpec((1, N_KV_HEADS, TM, QK_ROWS), lambda i: (i // nt, 0, i % nt, 0)),
            pl.BlockSpec((1, N_KV_HEADS, v_rows, TM), lambda i: (i // nt, 0, 0, i % nt)),
        ],
        out_shape=[
            jax.ShapeDtypeStruct((batch, N_HEADS, QK_ROWS, seq), BF16),
            jax.ShapeDtypeStruct((batch, N_KV_HEADS, seq, QK_ROWS), BF16),
            jax.ShapeDtypeStruct((batch, N_KV_HEADS, v_rows, seq), BF16),
        ],
        compiler_params=_params("parallel"),
        name="qkv_rope",
    )(x, gain, wT, cq, sq, ck, sk, qx, kx)


def _attn_fixed_kernel(qT_ref, k_ref, vT_ref, oT_ref):
    q = jnp.concatenate([qT_ref[0, g] for g in range(GQA_GROUP)], axis=1)
    s = jnp.dot(k_ref[0, 0], q, preferred_element_type=F32)
    p = jnp.exp2(s).astype(BF16)
    o = jnp.dot(vT_ref[0, 0], p, preferred_element_type=F32)
    o = o[:HEAD_DIM] / o[HEAD_DIM:HEAD_DIM + 1]
    for g in range(GQA_GROUP):
        oT_ref[0, g * HEAD_DIM:(g + 1) * HEAD_DIM, :] = o[:, g * TQ:(g + 1) * TQ].astype(BF16)


def _attn_online_kernel(qT_ref, k_ref, vT_ref, oT_ref, s0_ref, s1_ref, m_ref, acc_ref, *, seq):
    n_chunks = seq // TK

    def scores(c, s_ref):
        q = jnp.concatenate([qT_ref[0, g] for g in range(GQA_GROUP)], axis=1)
        kc = k_ref[0, 0, pl.ds(pl.multiple_of(c * TK, TK), TK), :]
        s_ref[...] = jnp.dot(kc, q, preferred_element_type=F32)

    def softmax_pv(c, s_ref):
        s = s_ref[...]
        m_prev = m_ref[...]
        m_new = jnp.maximum(m_prev, jnp.max(s, axis=0, keepdims=True))
        alpha = jnp.exp2(m_prev - m_new)
        p = jnp.exp2(s - m_new).astype(BF16)
        vc = vT_ref[0, 0, :, pl.ds(pl.multiple_of(c * TK, TK), TK)]
        acc_ref[...] = alpha * acc_ref[...] + jnp.dot(vc, p, preferred_element_type=F32)
        m_ref[...] = m_new

    m_ref[...] = jnp.full(m_ref.shape, -1e30, F32)
    acc_ref[...] = jnp.zeros(acc_ref.shape, F32)
    scores(0, s0_ref)

    def body(i, carry):
        c = 2 * i
        scores(c + 1, s1_ref)
        softmax_pv(c, s0_ref)
        scores(c + 2, s0_ref)
        softmax_pv(c + 1, s1_ref)
        return carry

    lax.fori_loop(0, n_chunks // 2 - 1, body, 0)
    scores(n_chunks - 1, s1_ref)
    softmax_pv(n_chunks - 2, s0_ref)
    softmax_pv(n_chunks - 1, s1_ref)
    acc = acc_ref[...]
    o = acc[:HEAD_DIM] / acc[HEAD_DIM:HEAD_DIM + 1]
    for g in range(GQA_GROUP):
        oT_ref[0, g * HEAD_DIM:(g + 1) * HEAD_DIM, :] = o[:, g * TQ:(g + 1) * TQ].astype(BF16)


def _attn_call(qT, k, vT, batch, seq, online):
    grp = GQA_GROUP * HEAD_DIM
    lanes = GQA_GROUP * TQ
    v_rows = HEAD_DIM + SUM_ROWS
    if online:
        body = functools.partial(_attn_online_kernel, seq=seq)
        scratch = [
            pltpu.VMEM((TK, lanes), F32),
            pltpu.VMEM((TK, lanes), F32),
            pltpu.VMEM((1, lanes), F32),
            pltpu.VMEM((v_rows, lanes), F32),
        ]
    else:
        body, scratch = _attn_fixed_kernel, []
    return pl.pallas_call(
        body,
        grid=(batch, N_KV_HEADS, seq // TQ),
        in_specs=[
            pl.BlockSpec((1, GQA_GROUP, QK_ROWS, TQ), lambda b, h, i: (b, h, 0, i)),
            pl.BlockSpec((1, 1, seq, QK_ROWS), lambda b, h, i: (b, h, 0, 0)),
            pl.BlockSpec((1, 1, v_rows, seq), lambda b, h, i: (b, h, 0, 0)),
        ],
        out_specs=pl.BlockSpec((1, grp, TQ), lambda b, h, i: (b, h, i)),
        out_shape=jax.ShapeDtypeStruct((batch, Q_DIM, seq), BF16),
        scratch_shapes=scratch,
        compiler_params=_params("parallel", "parallel", "parallel"),
        name="axial_attention_online" if online else "axial_attention",
    )(qT, k, vT)


def _memkv_kernel(mem_ref, g_ref, w_ref, kT_ref, v_ref):
    h = _rms(mem_ref[0], g_ref[0])
    kv = jnp.dot(h.astype(BF16), w_ref[0], preferred_element_type=F32)
    kT_ref[0, 0] = kv[:, :D_MODEL].T.astype(BF16)
    v_ref[0, 0] = kv[:, D_MODEL:].astype(BF16)


def _memkv_call(mem, gain, w, depth, batch):
    return pl.pallas_call(
        _memkv_kernel,
        grid=(depth, batch),
        in_specs=[
            pl.BlockSpec((1, MEM_LEN, D_MODEL), lambda l, b: (b, 0, 0)),
            pl.BlockSpec((1, 1, D_MODEL), lambda l, b: (l, 0, 0)),
            pl.BlockSpec((1, D_MODEL, 2 * D_MODEL), lambda l, b: (l, 0, 0)),
        ],
        out_specs=[
            pl.BlockSpec((1, 1, D_MODEL, MEM_LEN), lambda l, b: (l, b, 0, 0)),
            pl.BlockSpec((1, 1, MEM_LEN, D_MODEL), lambda l, b: (l, b, 0, 0)),
        ],
        out_shape=[
            jax.ShapeDtypeStruct((depth, batch, D_MODEL, MEM_LEN), BF16),
            jax.ShapeDtypeStruct((depth, batch, MEM_LEN, D_MODEL), BF16),
        ],
        compiler_params=_params("parallel", "parallel"),
        name="memory_kv",
    )(mem, gain, w)


def _xattn(x1, gx_ref, wq_ref, kT_ref, v_ref, wo_ref):
    h = _rms(x1, gx_ref[...])
    q = jnp.dot(h.astype(BF16), wq_ref[...], preferred_element_type=F32)
    q = q * ((X_HEAD_DIM ** -0.5) * LOG2E)
    outs = []
    for hd in range(X_HEADS):
        sl = slice(hd * X_HEAD_DIM, (hd + 1) * X_HEAD_DIM)
        s = jnp.dot(q[:, sl].astype(BF16), kT_ref[0, 0, sl, :],
                    preferred_element_type=F32)
        p = jnp.exp2(s - jnp.max(s, axis=-1, keepdims=True))
        l = jnp.sum(p, axis=-1, keepdims=True)
        o = jnp.dot(p.astype(BF16), v_ref[0, 0, :, sl], preferred_element_type=F32)
        outs.append((o / l).astype(BF16))
    o = jnp.concatenate(outs, axis=1)
    return x1 + jnp.dot(o, wo_ref[...], preferred_element_type=F32)


def _attn_out_xattn_kernel(x_ref, oT_ref, wo_ref, gx_ref, wq_ref, kT_ref, v_ref, wo2_ref,
                           out_ref):
    o = oT_ref[0].T
    x1 = x_ref[...] + jnp.dot(o, wo_ref[...], preferred_element_type=F32)
    out_ref[...] = _xattn(x1, gx_ref, wq_ref, kT_ref, v_ref, wo2_ref)


def _xattn_specs(layer, nt):
    return [
        _const_spec((1, D_MODEL)),
        _const_spec((D_MODEL, D_MODEL)),
        pl.BlockSpec((1, 1, D_MODEL, MEM_LEN), lambda i: (layer, i // nt, 0, 0)),
        pl.BlockSpec((1, 1, MEM_LEN, D_MODEL), lambda i: (layer, i // nt, 0, 0)),
        _const_spec((D_MODEL, D_MODEL)),
    ]


def _attn_out_xattn_call(x, oT, wo, gx, wq, kT, v, wo2, layer, batch, seq):
    nt = seq // TM
    return pl.pallas_call(
        _attn_out_xattn_kernel,
        grid=(batch * nt,),
        in_specs=[
            pl.BlockSpec((TM, D_MODEL), lambda i: (i, 0)),
            pl.BlockSpec((1, Q_DIM, TM), lambda i: (i // nt, 0, i % nt)),
            _const_spec((Q_DIM, D_MODEL)),
        ] + _xattn_specs(layer, nt),
        out_specs=pl.BlockSpec((TM, D_MODEL), lambda i: (i, 0)),
        out_shape=jax.ShapeDtypeStruct(x.shape, F32),
        compiler_params=_params("parallel"),
        name="attn_out_xattn",
    )(x, oT, wo, gx, wq, kT, v, wo2)


def _normed_with_halo(xp_ref, x_ref, xn_ref, gain, nt):
    i = pl.program_id(0)
    keep_prev = (i % nt != 0).astype(F32)
    keep_next = (i % nt != nt - 1).astype(F32)
    hp = _rms(xp_ref[...], gain) * keep_prev
    hn = _rms(xn_ref[...], gain) * keep_next
    return jnp.concatenate([hp, _rms(x_ref[...], gain), hn], axis=0)


def _halo_in_specs(n_tiles, tm):
    blocks = tm // HALO
    n_halo_blocks = n_tiles * blocks
    return [
        pl.BlockSpec((HALO, D_MODEL), lambda i: (jnp.maximum(i * blocks - 1, 0), 0)),
        pl.BlockSpec((tm, D_MODEL), lambda i: (i, 0)),
        pl.BlockSpec((HALO, D_MODEL),
                     lambda i: (jnp.minimum((i + 1) * blocks, n_halo_blocks - 1), 0)),
    ]


def _pool_xattn_kernel(xp_ref, x_ref, xn_ref, gp_ref, band_ref, wp_ref, sc_ref,
                       gx_ref, wq_ref, kT_ref, v_ref, wo2_ref, out_ref, *, nt, seq):
    i = pl.program_id(0)
    h = _normed_with_halo(xp_ref, x_ref, xn_ref, gp_ref[...], nt)
    hb = h.astype(BF16)
    pos = (i % nt) * TM + lax.broadcasted_iota(jnp.int32, (TM, 1), 0)
    ys = []
    for g, w in enumerate(POOL_WINDOWS):
        sl = slice(g * POOL_GROUP_W, (g + 1) * POOL_GROUP_W)
        cnt = (jnp.minimum(pos + w // 2, seq) - jnp.maximum(pos - w // 2, 0)).astype(F32)
        sums = jnp.concatenate(
            [jnp.dot(band_ref[g], hb[r:r + POOL_ROWS + 2 * HALO, sl], preferred_element_type=F32)
             for r in range(0, TM, POOL_ROWS)], axis=0)
        mixed = sums * (1.0 / cnt) - h[HALO:HALO + TM, sl]
        ys.append(jnp.dot(mixed.astype(BF16), wp_ref[g], preferred_element_type=F32))
    y = jnp.concatenate(ys, axis=1) * sc_ref[...]
    x1 = x_ref[...] + y
    out_ref[...] = _xattn(x1, gx_ref, wq_ref, kT_ref, v_ref, wo2_ref)


def _pool_bands():
    i = jnp.arange(POOL_ROWS)[:, None]
    m = jnp.arange(POOL_ROWS + 2 * HALO)[None, :] - HALO
    return jnp.stack([((m >= i - w // 2) & (m < i + w // 2)).astype(BF16) for w in POOL_WINDOWS])


def _pool_xattn_call(x, gp, wp, sc, gx, wq, kT, v, wo2, layer, batch, seq):
    nt = seq // TM
    assert max(POOL_WINDOWS) // 2 <= HALO and TM % POOL_ROWS == 0
    return pl.pallas_call(
        functools.partial(_pool_xattn_kernel, nt=nt, seq=seq),
        grid=(batch * nt,),
        in_specs=_halo_in_specs(batch * nt, TM) + [
            _const_spec((1, D_MODEL)),
            _const_spec((len(POOL_WINDOWS), POOL_ROWS, POOL_ROWS + 2 * HALO)),
            _const_spec((len(POOL_WINDOWS), POOL_GROUP_W, POOL_GROUP_W)),
            _const_spec((1, D_MODEL)),
        ] + _xattn_specs(layer, nt),
        out_specs=pl.BlockSpec((TM, D_MODEL), lambda i: (i, 0)),
        out_shape=jax.ShapeDtypeStruct(x.shape, F32),
        compiler_params=_params("parallel"),
        name="pool_xattn",
    )(x, x, x, gp, _pool_bands(), wp, sc, gx, wq, kT, v, wo2)


def _ffn_kernel(xp_ref, x_ref, xn_ref, g_ref, wup_ref, cw_ref, cb_ref, wdn_ref, gf_ref,
                out_ref, h_ref, act_ref, *, nt, final_norm):
    tm = x_ref.shape[0]
    rows = tm + 2 * HALO
    h_ref[...] = _normed_with_halo(xp_ref, x_ref, xn_ref, g_ref[...], nt).astype(BF16)

    def conv_up(off):
        u = _dot_rows(h_ref[...], wup_ref[:, off:off + FF_CHUNK], FFN_ROW_BLOCKS)
        cw = cw_ref[:, off:off + FF_CHUNK]
        c = (pltpu.roll(u, 1, 0) * cw[0:1] + u * cw[1:2]
             + pltpu.roll(u, rows - 1, 0) * cw[2:3] + cb_ref[:, off:off + FF_CHUNK])
        return c[HALO:HALO + tm]

    for c in range(D_FF // FF_CHUNK):
        gate = conv_up(c * FF_CHUNK)
        val = conv_up(D_FF + c * FF_CHUNK)
        act = gate * (1.0 / (1.0 + jnp.exp(-gate))) * val
        act_ref[:, c * FF_CHUNK:(c + 1) * FF_CHUNK] = act.astype(BF16)
    y = x_ref[...] + jnp.dot(act_ref[...], wdn_ref[...], preferred_element_type=F32)
    if final_norm:
        y = _rms(y, gf_ref[...])
    out_ref[...] = y


def _ffn_call(x, gain, wup, cw, cb, wdn, gf, final_norm, batch, seq):
    nt = seq // TM_FFN
    assert (TM_FFN + 2 * HALO) % (FFN_ROW_BLOCKS * HALO) == 0
    return pl.pallas_call(
        functools.partial(_ffn_kernel, nt=nt, final_norm=final_norm),
        grid=(batch * nt,),
        in_specs=_halo_in_specs(batch * nt, TM_FFN) + [
            _const_spec((1, D_MODEL)),
            _const_spec((D_MODEL, 2 * D_FF)),
            _const_spec((3, 2 * D_FF)),
            _const_spec((1, 2 * D_FF)),
            _const_spec((D_FF, D_MODEL)),
            _const_spec((1, D_MODEL)),
        ],
        out_specs=pl.BlockSpec((TM_FFN, D_MODEL), lambda i: (i, 0)),
        out_shape=jax.ShapeDtypeStruct(x.shape, F32),
        scratch_shapes=[
            pltpu.VMEM((TM_FFN + 2 * HALO, D_MODEL), BF16),
            pltpu.VMEM((TM_FFN, D_FF), BF16),
        ],
        compiler_params=_params("parallel"),
        name="conv_ffn",
    )(x, x, x, gain, wup, cw, cb, wdn, gf)


def _rope_tables(seq, gain, scale):
    t = jnp.arange(seq, dtype=jnp.int32)
    row = (t // GRID_W).astype(F32)
    col = (t % GRID_W).astype(F32)
    inv_freq = ROPE_THETA ** (-jnp.arange(ROPE_PAIRS, dtype=F32) / ROPE_PAIRS)
    ang_row = inv_freq[:, None] * row[None, :]
    ang_col = inv_freq[:, None] * col[None, :]
    ang = jnp.concatenate([ang_row, ang_row, ang_col, ang_col], axis=0)
    sign = jnp.concatenate([-jnp.ones((ROPE_PAIRS,), F32), jnp.ones((ROPE_PAIRS,), F32)] * 2)
    g = gain.astype(F32) * scale
    g_partner = _rope_rows(g[:, None])[:, 0]
    return jnp.cos(ang) * g[:, None], jnp.sin(ang) * (sign * g_partner)[:, None]


def kernel(x, mem, attn_norm, attn_w_qkv, attn_q_gain, attn_k_gain, attn_w_o, pool_norm, pool_w, pool_scale, xattn_norm, mem_norm, xattn_w_q, xattn_w_kv, xattn_w_o, ffn_norm, ffn_w_up, ffn_conv_w, ffn_conv_b, ffn_w_down, final_norm):
    batch, seq, d = x.shape
    depth = xattn_norm.shape[0]
    assert d == D_MODEL and seq % TM == 0 and seq % TM_FFN == 0 and seq % TQ == 0 and seq % TK == 0
    xf = x.reshape(batch * seq, d)

    kT_mem, v_mem = _memkv_call(mem, mem_norm.reshape(depth, 1, d),
                                xattn_w_kv.astype(BF16), depth, batch)

    ia = ib = 0
    for i in range(depth):
        xa = (xattn_norm[i].reshape(1, d), xattn_w_q[i].astype(BF16), kT_mem, v_mem,
              xattn_w_o[i].astype(BF16))
        if i % 2 == 0:
            bound = (HEAD_DIM ** 0.5) * LOG2E * jnp.max(jnp.abs(attn_q_gain[ia])) * jnp.max(
                jnp.abs(attn_k_gain[ia]))
            fixed_ok = bound <= MAX_STATIC_SHIFT
            shift = jnp.where(fixed_ok, bound, 0.0)
            qx = jnp.zeros((QK_ROWS - HEAD_DIM, TM), F32).at[0].set(-shift).astype(BF16)
            kx = jnp.zeros((TM, QK_ROWS - HEAD_DIM), BF16).at[:, 0].set(1.0)
            cq, sq = _rope_tables(seq, attn_q_gain[ia], (HEAD_DIM ** -0.5) * LOG2E)
            ck, sk = _rope_tables(seq, attn_k_gain[ia], 1.0)
            qT, k, vT = _qkv_call(
                xf, attn_norm[ia].reshape(1, d), attn_w_qkv[ia].T.astype(BF16),
                cq, sq, ck, sk, qx, kx, batch, seq)
            oT = lax.cond(
                fixed_ok,
                functools.partial(_attn_call, batch=batch, seq=seq, online=False),
                functools.partial(_attn_call, batch=batch, seq=seq, online=True),
                qT, k, vT)
            xf = _attn_out_xattn_call(xf, oT, attn_w_o[ia].astype(BF16), *xa, i, batch, seq)
            ia += 1
        else:
            xf = _pool_xattn_call(xf, pool_norm[ib].reshape(1, d), pool_w[ib].astype(BF16),
                                  pool_scale[ib].reshape(1, d), *xa, i, batch, seq)
            ib += 1
        xf = _ffn_call(xf, ffn_norm[i].reshape(1, d), ffn_w_up[i].astype(BF16),
                       ffn_conv_w[i], ffn_conv_b[i].reshape(1, 2 * D_FF),
                       ffn_w_down[i].astype(BF16), final_norm.reshape(1, d),
                       i == depth - 1, batch, seq)
    return xf.reshape(batch, seq, d)
```

```python
import functools
import math

import jax
import jax.numpy as jnp
from jax import lax
from jax.experimental import pallas as pl
from jax.experimental.pallas import tpu as pltpu

F32 = jnp.float32
BF16 = jnp.bfloat16

D_MODEL = 1024
GRID_W = 64
EPS = 1e-6
HEAD_DIM = 64
N_HEADS = 16
N_KV_HEADS = 4
GQA_GROUP = 4
ROPE_THETA = 10000.0
ROPE_PAIRS = 16
Q_DIM = N_HEADS * HEAD_DIM
KV_DIM = N_KV_HEADS * HEAD_DIM
QK_DIM = Q_DIM + KV_DIM
QKV_DIM = Q_DIM + 2 * KV_DIM
POOL_WINDOWS = (2, 4, 8, 16)
POOL_GROUP_W = 256
MEM_LEN = 256
X_HEADS = 4
X_HEAD_DIM = 256
D_FF = 2816
LOG2E = math.log2(math.e)

HALO = 16
SUM_ROWS = 16
VMEM_LIMIT = 56 * 1024 * 1024

TM = 1024
TM_FFN = 1024
TQ = 512
TK = 256
QK_ROWS = 2 * HEAD_DIM
MAX_STATIC_SHIFT = 40.0
FF_CHUNK = 256
POOL_ROWS = 128
MEMKV_BATCHES = 4
QKV_ROW_BLOCKS = 6
FFN_ROW_BLOCKS = 3


def _rms(x, gain):
    return x * lax.rsqrt(jnp.mean(x * x, axis=-1, keepdims=True) + EPS) * gain


def _params(*sem):
    return pltpu.CompilerParams(dimension_semantics=sem, vmem_limit_bytes=VMEM_LIMIT)


def _dot_rows(a, b, blocks):
    m = a.shape[0] // blocks
    return jnp.concatenate(
        [jnp.dot(a[r * m:(r + 1) * m], b, preferred_element_type=F32) for r in range(blocks)],
        axis=0)


def _const_spec(shape):
    nd = len(shape)
    return pl.BlockSpec(shape, lambda *_: (0,) * nd)


def _rope_rows(y):
    p = ROPE_PAIRS
    return jnp.concatenate([y[p:2 * p], y[:p], y[3 * p:], y[2 * p:3 * p]], axis=0)


def _qkv_kernel(x_ref, g_ref, wT_ref, cq_ref, sq_ref, ck_ref, sk_ref, qx_ref, kx_ref,
                qT_ref, k_ref, vT_ref):
    h = _rms(x_ref[...], g_ref[...])
    yT = _dot_rows(wT_ref[...], h.T.astype(BF16), QKV_ROW_BLOCKS)

    def norm_rope(hd, cos, sin):
        y = yT[hd * HEAD_DIM:(hd + 1) * HEAD_DIM]
        r = lax.rsqrt(jnp.mean(y * y, axis=0, keepdims=True) + EPS)
        return (y * cos + _rope_rows(y) * sin) * r

    cq, sq = cq_ref[...], sq_ref[...]
    for hd in range(N_HEADS):
        qT_ref[0, hd, :HEAD_DIM, :] = norm_rope(hd, cq, sq).astype(BF16)
        qT_ref[0, hd, HEAD_DIM:, :] = qx_ref[...]
    ck, sk = ck_ref[...], sk_ref[...]
    kT = jnp.concatenate([norm_rope(N_HEADS + g, ck, sk) for g in range(N_KV_HEADS)], axis=0)
    k = kT.T.astype(BF16)
    ones = jnp.ones((SUM_ROWS, yT.shape[1]), BF16)
    for g in range(N_KV_HEADS):
        sl = slice(g * HEAD_DIM, (g + 1) * HEAD_DIM)
        k_ref[0, g] = jnp.concatenate([k[:, sl], kx_ref[...]], axis=1)
        vT = yT[QK_DIM + g * HEAD_DIM:QK_DIM + (g + 1) * HEAD_DIM].astype(BF16)
        vT_ref[0, g] = jnp.concatenate([vT, ones], axis=0)


def _qkv_call(x, gain, wT, cq, sq, ck, sk, qx, kx, batch, seq):
    nt = seq // TM
    v_rows = HEAD_DIM + SUM_ROWS
    table = pl.BlockSpec((HEAD_DIM, TM), lambda i: (0, i % nt))
    return pl.pallas_call(
        _qkv_kernel,
        grid=(batch * nt,),
        in_specs=[
            pl.BlockSpec((TM, D_MODEL), lambda i: (i, 0)),
            _const_spec((1, D_MODEL)),
            _const_spec((QKV_DIM, D_MODEL)),
            table, table, table, table,
            _const_spec((QK_ROWS - HEAD_DIM, TM)),
            _const_spec((TM, QK_ROWS - HEAD_DIM)),
        ],
        out_specs=[
            pl.BlockSpec((1, N_HEADS, QK_ROWS, TM), lambda i: (i // nt, 0, 0, i % nt)),
            pl.BlockSpec((1, N_KV_HEADS, TM, QK_ROWS), lambda i: (i // nt, 0, i % nt, 0)),
            pl.BlockSpec((1, N_KV_HEADS, v_rows, TM), lambda i: (i // nt, 0, 0, i % nt)),
        ],
        out_shape=[
            jax.ShapeDtypeStruct((batch, N_HEADS, QK_ROWS, seq), BF16),
            jax.ShapeDtypeStruct((batch, N_KV_HEADS, seq, QK_ROWS), BF16),
            jax.ShapeDtypeStruct((batch, N_KV_HEADS, v_rows, seq), BF16),
        ],
        compiler_params=_params("parallel"),
        name="qkv_rope",
    )(x, gain, wT, cq, sq, ck, sk, qx, kx)


def _attn_fixed_kernel(qT_ref, k_ref, vT_ref, oT_ref):
    q = jnp.concatenate([qT_ref[0, g] for g in range(GQA_GROUP)], axis=1)
    s = jnp.dot(k_ref[0, 0], q, preferred_element_type=F32)
    p = jnp.exp2(s).astype(BF16)
    o = jnp.dot(vT_ref[0, 0], p, preferred_element_type=F32)
    o = o[:HEAD_DIM] / o[HEAD_DIM:HEAD_DIM + 1]
    for g in range(GQA_GROUP):
        oT_ref[0, g * HEAD_DIM:(g + 1) * HEAD_DIM, :] = o[:, g * TQ:(g + 1) * TQ].astype(BF16)


def _attn_online_kernel(qT_ref, k_ref, vT_ref, oT_ref, s0_ref, s1_ref, m_ref, acc_ref, *, seq):
    n_chunks = seq // TK

    def scores(c, s_ref):
        q = jnp.concatenate([qT_ref[0, g] for g in range(GQA_GROUP)], axis=1)
        kc = k_ref[0, 0, pl.ds(pl.multiple_of(c * TK, TK), TK), :]
        s_ref[...] = jnp.dot(kc, q, preferred_element_type=F32)

    def softmax_pv(c, s_ref):
        s = s_ref[...]
        m_prev = m_ref[...]
        m_new = jnp.maximum(m_prev, jnp.max(s, axis=0, keepdims=True))
        alpha = jnp.exp2(m_prev - m_new)
        p = jnp.exp2(s - m_new).astype(BF16)
        vc = vT_ref[0, 0, :, pl.ds(pl.multiple_of(c * TK, TK), TK)]
        acc_ref[...] = alpha * acc_ref[...] + jnp.dot(vc, p, preferred_element_type=F32)
        m_ref[...] = m_new

    m_ref[...] = jnp.full(m_ref.shape, -1e30, F32)
    acc_ref[...] = jnp.zeros(acc_ref.shape, F32)
    scores(0, s0_ref)

    def body(i, carry):
        c = 2 * i
        scores(c + 1, s1_ref)
        softmax_pv(c, s0_ref)
        scores(c + 2, s0_ref)
        softmax_pv(c + 1, s1_ref)
        return carry

    lax.fori_loop(0, n_chunks // 2 - 1, body, 0)
    scores(n_chunks - 1, s1_ref)
    softmax_pv(n_chunks - 2, s0_ref)
    softmax_pv(n_chunks - 1, s1_ref)
    acc = acc_ref[...]
    o = acc[:HEAD_DIM] / acc[HEAD_DIM:HEAD_DIM + 1]
    for g in range(GQA_GROUP):
        oT_ref[0, g * HEAD_DIM:(g + 1) * HEAD_DIM, :] = o[:, g * TQ:(g + 1) * TQ].astype(BF16)


def _attn_call(qT, k, vT, batch, seq, online):
    grp = GQA_GROUP * HEAD_DIM
    lanes = GQA_GROUP * TQ
    v_rows = HEAD_DIM + SUM_ROWS
    if online:
        body = functools.partial(_attn_online_kernel, seq=seq)
        scratch = [
            pltpu.VMEM((TK, lanes), F32),
            pltpu.VMEM((TK, lanes), F32),
            pltpu.VMEM((1, lanes), F32),
            pltpu.VMEM((v_rows, lanes), F32),
        ]
    else:
        body, scratch = _attn_fixed_kernel, []
    return pl.pallas_call(
        body,
        grid=(batch, N_KV_HEADS, seq // TQ),
        in_specs=[
            pl.BlockSpec((1, GQA_GROUP, QK_ROWS, TQ), lambda b, h, i: (b, h, 0, i)),
            pl.BlockSpec((1, 1, seq, QK_ROWS), lambda b, h, i: (b, h, 0, 0)),
            pl.BlockSpec((1, 1, v_rows, seq), lambda b, h, i: (b, h, 0, 0)),
        ],
        out_specs=pl.BlockSpec((1, grp, TQ), lambda b, h, i: (b, h, i)),
        out_shape=jax.ShapeDtypeStruct((batch, Q_DIM, seq), BF16),
        scratch_shapes=scratch,
        compiler_params=_params("parallel", "parallel", "parallel"),
        name="axial_attention_online" if online else "axial_attention",
    )(qT, k, vT)


def _memkv_kernel(mem_ref, g_ref, w_ref, kT_ref, v_ref):
    nb = mem_ref.shape[0]
    h = _rms(mem_ref[...].reshape(nb * MEM_LEN, D_MODEL), g_ref[0])
    kv = jnp.dot(h.astype(BF16), w_ref[0], preferred_element_type=F32)
    for b in range(nb):
        rows = slice(b * MEM_LEN, (b + 1) * MEM_LEN)
        kT_ref[0, b] = kv[rows, :D_MODEL].T.astype(BF16)
        v_ref[0, b] = kv[rows, D_MODEL:].astype(BF16)


def _memkv_call(mem, gain, w, depth, batch):
    nb = MEMKV_BATCHES
    assert batch % nb == 0
    return pl.pallas_call(
        _memkv_kernel,
        grid=(depth, batch // nb),
        in_specs=[
            pl.BlockSpec((nb, MEM_LEN, D_MODEL), lambda l, b: (b, 0, 0)),
            pl.BlockSpec((1, 1, D_MODEL), lambda l, b: (l, 0, 0)),
            pl.BlockSpec((1, D_MODEL, 2 * D_MODEL), lambda l, b: (l, 0, 0)),
        ],
        out_specs=[
            pl.BlockSpec((1, nb, D_MODEL, MEM_LEN), lambda l, b: (l, b, 0, 0)),
            pl.BlockSpec((1, nb, MEM_LEN, D_MODEL), lambda l, b: (l, b, 0, 0)),
        ],
        out_shape=[
            jax.ShapeDtypeStruct((depth, batch, D_MODEL, MEM_LEN), BF16),
            jax.ShapeDtypeStruct((depth, batch, MEM_LEN, D_MODEL), BF16),
        ],
        compiler_params=_params("parallel", "parallel"),
        name="memory_kv",
    )(mem, gain, w)


def _xattn(x1, gx_ref, wq_ref, kT_ref, v_ref, wo_ref):
    h = _rms(x1, gx_ref[...])
    q = jnp.dot(h.astype(BF16), wq_ref[0], preferred_element_type=F32)
    q = q * ((X_HEAD_DIM ** -0.5) * LOG2E)
    outs = []
    for hd in range(X_HEADS):
        sl = slice(hd * X_HEAD_DIM, (hd + 1) * X_HEAD_DIM)
        s = jnp.dot(q[:, sl].astype(BF16), kT_ref[0, 0, sl, :],
                    preferred_element_type=F32)
        p = jnp.exp2(s - jnp.max(s, axis=-1, keepdims=True))
        l = jnp.sum(p, axis=-1, keepdims=True)
        o = jnp.dot(p.astype(BF16), v_ref[0, 0, :, sl], preferred_element_type=F32)
        outs.append((o / l).astype(BF16))
    o = jnp.concatenate(outs, axis=1)
    return x1 + jnp.dot(o, wo_ref[0], preferred_element_type=F32)


def _attn_out_xattn_kernel(x_ref, oT_ref, wo_ref, gx_ref, wq_ref, kT_ref, v_ref, wo2_ref,
                           out_ref):
    o = oT_ref[0].T
    x1 = x_ref[...] + jnp.dot(o, wo_ref[...], preferred_element_type=F32)
    out_ref[...] = _xattn(x1, gx_ref, wq_ref, kT_ref, v_ref, wo2_ref)


def _layer_spec(shape, layer):
    nd = len(shape)
    return pl.BlockSpec((1,) + tuple(shape), lambda *_: (layer,) + (0,) * nd)


def _xattn_specs(layer, nt):
    return [
        _const_spec((1, D_MODEL)),
        _layer_spec((D_MODEL, D_MODEL), layer),
        pl.BlockSpec((1, 1, D_MODEL, MEM_LEN), lambda i: (layer, i // nt, 0, 0)),
        pl.BlockSpec((1, 1, MEM_LEN, D_MODEL), lambda i: (layer, i // nt, 0, 0)),
        _layer_spec((D_MODEL, D_MODEL), layer),
    ]


def _attn_out_xattn_call(x, oT, wo, gx, wq, kT, v, wo2, layer, batch, seq):
    nt = seq // TM
    return pl.pallas_call(
        _attn_out_xattn_kernel,
        grid=(batch * nt,),
        in_specs=[
            pl.BlockSpec((TM, D_MODEL), lambda i: (i, 0)),
            pl.BlockSpec((1, Q_DIM, TM), lambda i: (i // nt, 0, i % nt)),
            _const_spec((Q_DIM, D_MODEL)),
        ] + _xattn_specs(layer, nt),
        out_specs=pl.BlockSpec((TM, D_MODEL), lambda i: (i, 0)),
        out_shape=jax.ShapeDtypeStruct(x.shape, F32),
        compiler_params=_params("parallel"),
        name="attn_out_xattn",
    )(x, oT, wo, gx, wq, kT, v, wo2)


def _normed_with_halo(xp_ref, x_ref, xn_ref, gain, nt):
    i = pl.program_id(0)
    keep_prev = (i % nt != 0).astype(F32)
    keep_next = (i % nt != nt - 1).astype(F32)
    hp = _rms(xp_ref[...], gain) * keep_prev
    hn = _rms(xn_ref[...], gain) * keep_next
    return jnp.concatenate([hp, _rms(x_ref[...], gain), hn], axis=0)


def _halo_in_specs(n_tiles, tm):
    blocks = tm // HALO
    n_halo_blocks = n_tiles * blocks
    return [
        pl.BlockSpec((HALO, D_MODEL), lambda i: (jnp.maximum(i * blocks - 1, 0), 0)),
        pl.BlockSpec((tm, D_MODEL), lambda i: (i, 0)),
        pl.BlockSpec((HALO, D_MODEL),
                     lambda i: (jnp.minimum((i + 1) * blocks, n_halo_blocks - 1), 0)),
    ]


def _pool_xattn_kernel(xp_ref, x_ref, xn_ref, gp_ref, band_ref, wp_ref, sc_ref,
                       gx_ref, wq_ref, kT_ref, v_ref, wo2_ref, out_ref, *, nt, seq):
    i = pl.program_id(0)
    h = _normed_with_halo(xp_ref, x_ref, xn_ref, gp_ref[...], nt)
    hb = h.astype(BF16)
    pos = (i % nt) * TM + lax.broadcasted_iota(jnp.int32, (TM, 1), 0)
    ys = []
    for g, w in enumerate(POOL_WINDOWS):
        sl = slice(g * POOL_GROUP_W, (g + 1) * POOL_GROUP_W)
        cnt = (jnp.minimum(pos + w // 2, seq) - jnp.maximum(pos - w // 2, 0)).astype(F32)
        sums = jnp.concatenate(
            [jnp.dot(band_ref[g], hb[r:r + POOL_ROWS + 2 * HALO, sl], preferred_element_type=F32)
             for r in range(0, TM, POOL_ROWS)], axis=0)
        mixed = sums * (1.0 / cnt) - h[HALO:HALO + TM, sl]
        ys.append(jnp.dot(mixed.astype(BF16), wp_ref[g], preferred_element_type=F32))
    y = jnp.concatenate(ys, axis=1) * sc_ref[...]
    x1 = x_ref[...] + y
    out_ref[...] = _xattn(x1, gx_ref, wq_ref, kT_ref, v_ref, wo2_ref)


def _pool_bands():
    i = jnp.arange(POOL_ROWS)[:, None]
    m = jnp.arange(POOL_ROWS + 2 * HALO)[None, :] - HALO
    return jnp.stack([((m >= i - w // 2) & (m < i + w // 2)).astype(BF16) for w in POOL_WINDOWS])


def _pool_xattn_call(x, gp, wp, sc, gx, wq, kT, v, wo2, layer, batch, seq):
    nt = seq // TM
    assert max(POOL_WINDOWS) // 2 <= HALO and TM % POOL_ROWS == 0
    return pl.pallas_call(
        functools.partial(_pool_xattn_kernel, nt=nt, seq=seq),
        grid=(batch * nt,),
        in_specs=_halo_in_specs(batch * nt, TM) + [
            _const_spec((1, D_MODEL)),
            _const_spec((len(POOL_WINDOWS), POOL_ROWS, POOL_ROWS + 2 * HALO)),
            _const_spec((len(POOL_WINDOWS), POOL_GROUP_W, POOL_GROUP_W)),
            _const_spec((1, D_MODEL)),
        ] + _xattn_specs(layer, nt),
        out_specs=pl.BlockSpec((TM, D_MODEL), lambda i: (i, 0)),
        out_shape=jax.ShapeDtypeStruct(x.shape, F32),
        compiler_params=_params("parallel"),
        name="pool_xattn",
    )(x, x, x, gp, _pool_bands(), wp, sc, gx, wq, kT, v, wo2)


def _ffn_kernel(xp_ref, x_ref, xn_ref, g_ref, wup_ref, cw_ref, cb_ref, wdn_ref, gf_ref,
                out_ref, h_ref, act_ref, *, nt, final_norm):
    tm = x_ref.shape[0]
    rows = tm + 2 * HALO
    h_ref[...] = _normed_with_halo(xp_ref, x_ref, xn_ref, g_ref[...], nt).astype(BF16)

    def conv_up(off):
        u = _dot_rows(h_ref[...], wup_ref[0, :, off:off + FF_CHUNK], FFN_ROW_BLOCKS)
        cw = cw_ref[0, :, off:off + FF_CHUNK]
        c = (pltpu.roll(u, 1, 0) * cw[0:1] + u * cw[1:2]
             + pltpu.roll(u, rows - 1, 0) * cw[2:3] + cb_ref[0, :, off:off + FF_CHUNK])
        return c[HALO:HALO + tm]

    for c in range(D_FF // FF_CHUNK):
        gate = conv_up(c * FF_CHUNK)
        val = conv_up(D_FF + c * FF_CHUNK)
        act = gate * (1.0 / (1.0 + jnp.exp(-gate))) * val
        act_ref[:, c * FF_CHUNK:(c + 1) * FF_CHUNK] = act.astype(BF16)
    y = x_ref[...] + jnp.dot(act_ref[...], wdn_ref[0], preferred_element_type=F32)
    if final_norm:
        y = _rms(y, gf_ref[...])
    out_ref[...] = y


def _ffn_call(x, gain, wup, cw, cb, wdn, gf, layer, final_norm, batch, seq):
    nt = seq // TM_FFN
    assert (TM_FFN + 2 * HALO) % (FFN_ROW_BLOCKS * HALO) == 0
    return pl.pallas_call(
        functools.partial(_ffn_kernel, nt=nt, final_norm=final_norm),
        grid=(batch * nt,),
        in_specs=_halo_in_specs(batch * nt, TM_FFN) + [
            _const_spec((1, D_MODEL)),
            _layer_spec((D_MODEL, 2 * D_FF), layer),
            _layer_spec((3, 2 * D_FF), layer),
            _layer_spec((1, 2 * D_FF), layer),
            _layer_spec((D_FF, D_MODEL), layer),
            _const_spec((1, D_MODEL)),
        ],
        out_specs=pl.BlockSpec((TM_FFN, D_MODEL), lambda i: (i, 0)),
        out_shape=jax.ShapeDtypeStruct(x.shape, F32),
        scratch_shapes=[
            pltpu.VMEM((TM_FFN + 2 * HALO, D_MODEL), BF16),
            pltpu.VMEM((TM_FFN, D_FF), BF16),
        ],
        compiler_params=_params("parallel"),
        name="conv_ffn",
    )(x, x, x, gain, wup, cw, cb, wdn, gf)


def _rope_tables(seq, gain, scale):
    t = jnp.arange(seq, dtype=jnp.int32)
    row = (t // GRID_W).astype(F32)
    col = (t % GRID_W).astype(F32)
    inv_freq = ROPE_THETA ** (-jnp.arange(ROPE_PAIRS, dtype=F32) / ROPE_PAIRS)
    ang_row = inv_freq[:, None] * row[None, :]
    ang_col = inv_freq[:, None] * col[None, :]
    ang = jnp.concatenate([ang_row, ang_row, ang_col, ang_col], axis=0)
    sign = jnp.concatenate([-jnp.ones((ROPE_PAIRS,), F32), jnp.ones((ROPE_PAIRS,), F32)] * 2)
    g = gain.astype(F32) * scale
    g_partner = _rope_rows(g[:, None])[:, 0]
    return jnp.cos(ang) * g[:, None], jnp.sin(ang) * (sign * g_partner)[:, None]


def kernel(x, mem, attn_norm, attn_w_qkv, attn_q_gain, attn_k_gain, attn_w_o, pool_norm, pool_w, pool_scale, xattn_norm, mem_norm, xattn_w_q, xattn_w_kv, xattn_w_o, ffn_norm, ffn_w_up, ffn_conv_w, ffn_conv_b, ffn_w_down, final_norm):
    batch, seq, d = x.shape
    depth = xattn_norm.shape[0]
    assert d == D_MODEL and seq % TM == 0 and seq % TM_FFN == 0 and seq % TQ == 0 and seq % TK == 0
    xf = x.reshape(batch * seq, d)

    kT_mem, v_mem = _memkv_call(mem, mem_norm.reshape(depth, 1, d),
                                xattn_w_kv.astype(BF16), depth, batch)

    xw_q, xw_o = xattn_w_q.astype(BF16), xattn_w_o.astype(BF16)
    f_up, f_down = ffn_w_up.astype(BF16), ffn_w_down.astype(BF16)
    f_cb = ffn_conv_b.reshape(depth, 1, 2 * D_FF)

    ia = ib = 0
    for i in range(depth):
        xa = (xattn_norm[i].reshape(1, d), xw_q, kT_mem, v_mem, xw_o)
        if i % 2 == 0:
            bound = (HEAD_DIM ** 0.5) * LOG2E * jnp.max(jnp.abs(attn_q_gain[ia])) * jnp.max(
                jnp.abs(attn_k_gain[ia]))
            fixed_ok = bound <= MAX_STATIC_SHIFT
            shift = jnp.where(fixed_ok, bound, 0.0)
            qx = jnp.zeros((QK_ROWS - HEAD_DIM, TM), F32).at[0].set(-shift).astype(BF16)
            kx = jnp.zeros((TM, QK_ROWS - HEAD_DIM), BF16).at[:, 0].set(1.0)
            cq, sq = _rope_tables(seq, attn_q_gain[ia], (HEAD_DIM ** -0.5) * LOG2E)
            ck, sk = _rope_tables(seq, attn_k_gain[ia], 1.0)
            qT, k, vT = _qkv_call(
                xf, attn_norm[ia].reshape(1, d), attn_w_qkv[ia].T.astype(BF16),
                cq, sq, ck, sk, qx, kx, batch, seq)
            oT = lax.cond(
                fixed_ok,
                functools.partial(_attn_call, batch=batch, seq=seq, online=False),
                functools.partial(_attn_call, batch=batch, seq=seq, online=True),
                qT, k, vT)
            xf = _attn_out_xattn_call(xf, oT, attn_w_o[ia].astype(BF16), *xa, i, batch, seq)
            ia += 1
        else:
            xf = _pool_xattn_call(xf, pool_norm[ib].reshape(1, d), pool_w[ib].astype(BF16),
                                  pool_scale[ib].reshape(1, d), *xa, i, batch, seq)
            ib += 1
        xf = _ffn_call(xf, ffn_norm[i].reshape(1, d), f_up, ffn_conv_w, f_cb, f_down,
                       final_norm.reshape(1, d), i, i == depth - 1, batch, seq)
    return xf.reshape(batch, seq, d)
```

```python
import functools
import math

import jax
import jax.numpy as jnp
from jax import lax
from jax.experimental import pallas as pl
from jax.experimental.pallas import tpu as pltpu

F32 = jnp.float32
BF16 = jnp.bfloat16

D_MODEL = 1024
GRID_W = 64
EPS = 1e-6
HEAD_DIM = 64
N_HEADS = 16
N_KV_HEADS = 4
GQA_GROUP = 4
ROPE_THETA = 10000.0
ROPE_PAIRS = 16
Q_DIM = N_HEADS * HEAD_DIM
KV_DIM = N_KV_HEADS * HEAD_DIM
QK_DIM = Q_DIM + KV_DIM
QKV_DIM = Q_DIM + 2 * KV_DIM
POOL_WINDOWS = (2, 4, 8, 16)
POOL_GROUP_W = 256
MEM_LEN = 256
X_HEADS = 4
X_HEAD_DIM = 256
D_FF = 2816
LOG2E = math.log2(math.e)

HALO = 16
SUM_ROWS = 16
VMEM_LIMIT = 56 * 1024 * 1024

TM = 1024
TM_FFN = 1024
TQ = 512
TK = 256
QK_ROWS = 2 * HEAD_DIM
MAX_STATIC_SHIFT = 40.0
FF_CHUNK = 256
POOL_ROWS = 128
MEMKV_BATCHES = 4
QKV_ROW_BLOCKS = 6
FFN_ROW_BLOCKS = 3


def _rms(x, gain):
    return x * lax.rsqrt(jnp.mean(x * x, axis=-1, keepdims=True) + EPS) * gain


def _params(*sem):
    return pltpu.CompilerParams(dimension_semantics=sem, vmem_limit_bytes=VMEM_LIMIT)


def _dot_rows(a, b, blocks):
    m = a.shape[0] // blocks
    return jnp.concatenate(
        [jnp.dot(a[r * m:(r + 1) * m], b, preferred_element_type=F32) for r in range(blocks)],
        axis=0)


def _const_spec(shape):
    nd = len(shape)
    return pl.BlockSpec(shape, lambda *_: (0,) * nd)


def _rope_rows(y):
    p = ROPE_PAIRS
    return jnp.concatenate([y[p:2 * p], y[:p], y[3 * p:], y[2 * p:3 * p]], axis=0)


def _qkv_kernel(x_ref, g_ref, wT_ref, cq_ref, sq_ref, ck_ref, sk_ref, qx_ref, kx_ref,
                qT_ref, k_ref, vT_ref):
    h = _rms(x_ref[...], g_ref[...])
    yT = _dot_rows(wT_ref[...], h.T.astype(BF16), QKV_ROW_BLOCKS)

    def norm_rope(hd, cos, sin):
        y = yT[hd * HEAD_DIM:(hd + 1) * HEAD_DIM]
        r = lax.rsqrt(jnp.mean(y * y, axis=0, keepdims=True) + EPS)
        return (y * cos + _rope_rows(y) * sin) * r

    cq, sq = cq_ref[...], sq_ref[...]
    for hd in range(N_HEADS):
        qT_ref[0, hd, :HEAD_DIM, :] = norm_rope(hd, cq, sq).astype(BF16)
        qT_ref[0, hd, HEAD_DIM:, :] = qx_ref[...]
    ck, sk = ck_ref[...], sk_ref[...]
    kT = jnp.concatenate([norm_rope(N_HEADS + g, ck, sk) for g in range(N_KV_HEADS)], axis=0)
    k = kT.T.astype(BF16)
    ones = jnp.ones((SUM_ROWS, yT.shape[1]), BF16)
    for g in range(N_KV_HEADS):
        sl = slice(g * HEAD_DIM, (g + 1) * HEAD_DIM)
        k_ref[0, g] = jnp.concatenate([k[:, sl], kx_ref[...]], axis=1)
        vT = yT[QK_DIM + g * HEAD_DIM:QK_DIM + (g + 1) * HEAD_DIM].astype(BF16)
        vT_ref[0, g] = jnp.concatenate([vT, ones], axis=0)


def _qkv_call(x, gain, wT, cq, sq, ck, sk, qx, kx, batch, seq):
    nt = seq // TM
    v_rows = HEAD_DIM + SUM_ROWS
    table = pl.BlockSpec((HEAD_DIM, TM), lambda i: (0, i % nt))
    return pl.pallas_call(
        _qkv_kernel,
        grid=(batch * nt,),
        in_specs=[
            pl.BlockSpec((TM, D_MODEL), lambda i: (i, 0)),
            _const_spec((1, D_MODEL)),
            _const_spec((QKV_DIM, D_MODEL)),
            table, table, table, table,
            _const_spec((QK_ROWS - HEAD_DIM, TM)),
            _const_spec((TM, QK_ROWS - HEAD_DIM)),
        ],
        out_specs=[
            pl.BlockSpec((1, N_HEADS, QK_ROWS, TM), lambda i: (i // nt, 0, 0, i % nt)),
            pl.BlockSpec((1, N_KV_HEADS, TM, QK_ROWS), lambda i: (i // nt, 0, i % nt, 0)),
            pl.BlockSpec((1, N_KV_HEADS, v_rows, TM), lambda i: (i // nt, 0, 0, i % nt)),
        ],
        out_shape=[
            jax.ShapeDtypeStruct((batch, N_HEADS, QK_ROWS, seq), BF16),
            jax.ShapeDtypeStruct((batch, N_KV_HEADS, seq, QK_ROWS), BF16),
            jax.ShapeDtypeStruct((batch, N_KV_HEADS, v_rows, seq), BF16),
        ],
        compiler_params=_params("parallel"),
        name="qkv_rope",
    )(x, gain, wT, cq, sq, ck, sk, qx, kx)


def _attn_fixed_kernel(qT_ref, k_ref, vT_ref, oT_ref, p_ref):
    q = jnp.concatenate([qT_ref[0, g] for g in range(GQA_GROUP)], axis=1)
    s = jnp.dot(k_ref[0, 0], q, preferred_element_type=F32)
    seq = k_ref.shape[2]
    z = pl.multiple_of(jnp.minimum(pl.program_id(2), 0) * SUM_ROWS, SUM_ROWS)
    p_ref[pl.ds(z, seq), :] = jnp.exp2(s).astype(BF16)
    o = jnp.dot(vT_ref[0, 0], p_ref[pl.ds(z, seq), :], preferred_element_type=F32)
    o = o[:HEAD_DIM] / o[HEAD_DIM:HEAD_DIM + 1]
    for g in range(GQA_GROUP):
        oT_ref[0, g * HEAD_DIM:(g + 1) * HEAD_DIM, :] = o[:, g * TQ:(g + 1) * TQ].astype(BF16)


def _attn_online_kernel(qT_ref, k_ref, vT_ref, oT_ref, s0_ref, s1_ref, m_ref, acc_ref, *, seq):
    n_chunks = seq // TK

    def scores(c, s_ref):
        q = jnp.concatenate([qT_ref[0, g] for g in range(GQA_GROUP)], axis=1)
        kc = k_ref[0, 0, pl.ds(pl.multiple_of(c * TK, TK), TK), :]
        s_ref[...] = jnp.dot(kc, q, preferred_element_type=F32)

    def softmax_pv(c, s_ref):
        s = s_ref[...]
        m_prev = m_ref[...]
        m_new = jnp.maximum(m_prev, jnp.max(s, axis=0, keepdims=True))
        alpha = jnp.exp2(m_prev - m_new)
        p = jnp.exp2(s - m_new).astype(BF16)
        vc = vT_ref[0, 0, :, pl.ds(pl.multiple_of(c * TK, TK), TK)]
        acc_ref[...] = alpha * acc_ref[...] + jnp.dot(vc, p, preferred_element_type=F32)
        m_ref[...] = m_new

    m_ref[...] = jnp.full(m_ref.shape, -1e30, F32)
    acc_ref[...] = jnp.zeros(acc_ref.shape, F32)
    scores(0, s0_ref)

    def body(i, carry):
        c = 2 * i
        scores(c + 1, s1_ref)
        softmax_pv(c, s0_ref)
        scores(c + 2, s0_ref)
        softmax_pv(c + 1, s1_ref)
        return carry

    lax.fori_loop(0, n_chunks // 2 - 1, body, 0)
    scores(n_chunks - 1, s1_ref)
    softmax_pv(n_chunks - 2, s0_ref)
    softmax_pv(n_chunks - 1, s1_ref)
    acc = acc_ref[...]
    o = acc[:HEAD_DIM] / acc[HEAD_DIM:HEAD_DIM + 1]
    for g in range(GQA_GROUP):
        oT_ref[0, g * HEAD_DIM:(g + 1) * HEAD_DIM, :] = o[:, g * TQ:(g + 1) * TQ].astype(BF16)


def _attn_call(qT, k, vT, batch, seq, online):
    grp = GQA_GROUP * HEAD_DIM
    lanes = GQA_GROUP * TQ
    v_rows = HEAD_DIM + SUM_ROWS
    if online:
        body = functools.partial(_attn_online_kernel, seq=seq)
        scratch = [
            pltpu.VMEM((TK, lanes), F32),
            pltpu.VMEM((TK, lanes), F32),
            pltpu.VMEM((1, lanes), F32),
            pltpu.VMEM((v_rows, lanes), F32),
        ]
    else:
        body, scratch = _attn_fixed_kernel, [pltpu.VMEM((seq + SUM_ROWS, lanes), BF16)]
    return pl.pallas_call(
        body,
        grid=(batch, N_KV_HEADS, seq // TQ),
        in_specs=[
            pl.BlockSpec((1, GQA_GROUP, QK_ROWS, TQ), lambda b, h, i: (b, h, 0, i)),
            pl.BlockSpec((1, 1, seq, QK_ROWS), lambda b, h, i: (b, h, 0, 0)),
            pl.BlockSpec((1, 1, v_rows, seq), lambda b, h, i: (b, h, 0, 0)),
        ],
        out_specs=pl.BlockSpec((1, grp, TQ), lambda b, h, i: (b, h, i)),
        out_shape=jax.ShapeDtypeStruct((batch, Q_DIM, seq), BF16),
        scratch_shapes=scratch,
        compiler_params=_params("parallel", "parallel", "parallel"),
        name="axial_attention_online" if online else "axial_attention",
    )(qT, k, vT)


def _memkv_kernel(mem_ref, g_ref, w_ref, kT_ref, v_ref):
    nb = mem_ref.shape[0]
    h = _rms(mem_ref[...].reshape(nb * MEM_LEN, D_MODEL), g_ref[0])
    kv = jnp.dot(h.astype(BF16), w_ref[0], preferred_element_type=F32)
    for b in range(nb):
        rows = slice(b * MEM_LEN, (b + 1) * MEM_LEN)
        kT_ref[0, b] = kv[rows, :D_MODEL].T.astype(BF16)
        v_ref[0, b] = kv[rows, D_MODEL:].astype(BF16)


def _memkv_call(mem, gain, w, depth, batch):
    nb = MEMKV_BATCHES
    assert batch % nb == 0
    return pl.pallas_call(
        _memkv_kernel,
        grid=(depth, batch // nb),
        in_specs=[
            pl.BlockSpec((nb, MEM_LEN, D_MODEL), lambda l, b: (b, 0, 0)),
            pl.BlockSpec((1, 1, D_MODEL), lambda l, b: (l, 0, 0)),
            pl.BlockSpec((1, D_MODEL, 2 * D_MODEL), lambda l, b: (l, 0, 0)),
        ],
        out_specs=[
            pl.BlockSpec((1, nb, D_MODEL, MEM_LEN), lambda l, b: (l, b, 0, 0)),
            pl.BlockSpec((1, nb, MEM_LEN, D_MODEL), lambda l, b: (l, b, 0, 0)),
        ],
        out_shape=[
            jax.ShapeDtypeStruct((depth, batch, D_MODEL, MEM_LEN), BF16),
            jax.ShapeDtypeStruct((depth, batch, MEM_LEN, D_MODEL), BF16),
        ],
        compiler_params=_params("parallel", "parallel"),
        name="memory_kv",
    )(mem, gain, w)


def _xattn(x1, gx_ref, wq_ref, kT_ref, v_ref, wo_ref):
    h = _rms(x1, gx_ref[...])
    q = jnp.dot(h.astype(BF16), wq_ref[0], preferred_element_type=F32)
    q = q * ((X_HEAD_DIM ** -0.5) * LOG2E)
    outs = []
    for hd in range(X_HEADS):
        sl = slice(hd * X_HEAD_DIM, (hd + 1) * X_HEAD_DIM)
        s = jnp.dot(q[:, sl].astype(BF16), kT_ref[0, 0, sl, :],
                    preferred_element_type=F32)
        p = jnp.exp2(s - jnp.max(s, axis=-1, keepdims=True))
        l = jnp.sum(p, axis=-1, keepdims=True)
        o = jnp.dot(p.astype(BF16), v_ref[0, 0, :, sl], preferred_element_type=F32)
        outs.append((o / l).astype(BF16))
    o = jnp.concatenate(outs, axis=1)
    return x1 + jnp.dot(o, wo_ref[0], preferred_element_type=F32)


def _attn_out_xattn_kernel(x_ref, oT_ref, wo_ref, gx_ref, wq_ref, kT_ref, v_ref, wo2_ref,
                           out_ref):
    o = oT_ref[0].T
    x1 = x_ref[...] + jnp.dot(o, wo_ref[...], preferred_element_type=F32)
    out_ref[...] = _xattn(x1, gx_ref, wq_ref, kT_ref, v_ref, wo2_ref)


def _layer_spec(shape, layer):
    nd = len(shape)
    return pl.BlockSpec((1,) + tuple(shape), lambda *_: (layer,) + (0,) * nd)


def _xattn_specs(layer, nt):
    return [
        _const_spec((1, D_MODEL)),
        _layer_spec((D_MODEL, D_MODEL), layer),
        pl.BlockSpec((1, 1, D_MODEL, MEM_LEN), lambda i: (layer, i // nt, 0, 0)),
        pl.BlockSpec((1, 1, MEM_LEN, D_MODEL), lambda i: (layer, i // nt, 0, 0)),
        _layer_spec((D_MODEL, D_MODEL), layer),
    ]


def _attn_out_xattn_call(x, oT, wo, gx, wq, kT, v, wo2, layer, batch, seq):
    nt = seq // TM
    return pl.pallas_call(
        _attn_out_xattn_kernel,
        grid=(batch * nt,),
        in_specs=[
            pl.BlockSpec((TM, D_MODEL), lambda i: (i, 0)),
            pl.BlockSpec((1, Q_DIM, TM), lambda i: (i // nt, 0, i % nt)),
            _const_spec((Q_DIM, D_MODEL)),
        ] + _xattn_specs(layer, nt),
        out_specs=pl.BlockSpec((TM, D_MODEL), lambda i: (i, 0)),
        out_shape=jax.ShapeDtypeStruct(x.shape, F32),
        compiler_params=_params("parallel"),
        name="attn_out_xattn",
    )(x, oT, wo, gx, wq, kT, v, wo2)


def _normed_with_halo(xp_ref, x_ref, xn_ref, gain, nt):
    i = pl.program_id(0)
    keep_prev = (i % nt != 0).astype(F32)
    keep_next = (i % nt != nt - 1).astype(F32)
    hp = _rms(xp_ref[...], gain) * keep_prev
    hn = _rms(xn_ref[...], gain) * keep_next
    return jnp.concatenate([hp, _rms(x_ref[...], gain), hn], axis=0)


def _halo_in_specs(n_tiles, tm):
    blocks = tm // HALO
    n_halo_blocks = n_tiles * blocks
    return [
        pl.BlockSpec((HALO, D_MODEL), lambda i: (jnp.maximum(i * blocks - 1, 0), 0)),
        pl.BlockSpec((tm, D_MODEL), lambda i: (i, 0)),
        pl.BlockSpec((HALO, D_MODEL),
                     lambda i: (jnp.minimum((i + 1) * blocks, n_halo_blocks - 1), 0)),
    ]


def _pool_xattn_kernel(xp_ref, x_ref, xn_ref, gp_ref, band_ref, wp_ref, sc_ref,
                       gx_ref, wq_ref, kT_ref, v_ref, wo2_ref, out_ref, *, nt, seq):
    i = pl.program_id(0)
    h = _normed_with_halo(xp_ref, x_ref, xn_ref, gp_ref[...], nt)
    hb = h.astype(BF16)
    pos = (i % nt) * TM + lax.broadcasted_iota(jnp.int32, (TM, 1), 0)
    ys = []
    for g, w in enumerate(POOL_WINDOWS):
        sl = slice(g * POOL_GROUP_W, (g + 1) * POOL_GROUP_W)
        cnt = (jnp.minimum(pos + w // 2, seq) - jnp.maximum(pos - w // 2, 0)).astype(F32)
        sums = jnp.concatenate(
            [jnp.dot(band_ref[g], hb[r:r + POOL_ROWS + 2 * HALO, sl], preferred_element_type=F32)
             for r in range(0, TM, POOL_ROWS)], axis=0)
        mixed = sums * (1.0 / cnt) - h[HALO:HALO + TM, sl]
        ys.append(jnp.dot(mixed.astype(BF16), wp_ref[g], preferred_element_type=F32))
    y = jnp.concatenate(ys, axis=1) * sc_ref[...]
    x1 = x_ref[...] + y
    out_ref[...] = _xattn(x1, gx_ref, wq_ref, kT_ref, v_ref, wo2_ref)


def _pool_bands():
    i = jnp.arange(POOL_ROWS)[:, None]
    m = jnp.arange(POOL_ROWS + 2 * HALO)[None, :] - HALO
    return jnp.stack([((m >= i - w // 2) & (m < i + w // 2)).astype(BF16) for w in POOL_WINDOWS])


def _pool_xattn_call(x, gp, wp, sc, gx, wq, kT, v, wo2, layer, batch, seq):
    nt = seq // TM
    assert max(POOL_WINDOWS) // 2 <= HALO and TM % POOL_ROWS == 0
    return pl.pallas_call(
        functools.partial(_pool_xattn_kernel, nt=nt, seq=seq),
        grid=(batch * nt,),
        in_specs=_halo_in_specs(batch * nt, TM) + [
            _const_spec((1, D_MODEL)),
            _const_spec((len(POOL_WINDOWS), POOL_ROWS, POOL_ROWS + 2 * HALO)),
            _const_spec((len(POOL_WINDOWS), POOL_GROUP_W, POOL_GROUP_W)),
            _const_spec((1, D_MODEL)),
        ] + _xattn_specs(layer, nt),
        out_specs=pl.BlockSpec((TM, D_MODEL), lambda i: (i, 0)),
        out_shape=jax.ShapeDtypeStruct(x.shape, F32),
        compiler_params=_params("parallel"),
        name="pool_xattn",
    )(x, x, x, gp, _pool_bands(), wp, sc, gx, wq, kT, v, wo2)


def _ffn_kernel(xp_ref, x_ref, xn_ref, g_ref, wup_ref, cw_ref, cb_ref, wdn_ref, gf_ref,
                out_ref, h_ref, act_ref, *, nt, final_norm):
    tm = x_ref.shape[0]
    rows = tm + 2 * HALO
    h_ref[...] = _normed_with_halo(xp_ref, x_ref, xn_ref, g_ref[...], nt).astype(BF16)

    def conv_up(off):
        u = _dot_rows(h_ref[...], wup_ref[0, :, off:off + FF_CHUNK], FFN_ROW_BLOCKS)
        cw = cw_ref[0, :, off:off + FF_CHUNK]
        c = (pltpu.roll(u, 1, 0) * cw[0:1] + u * cw[1:2]
             + pltpu.roll(u, rows - 1, 0) * cw[2:3] + cb_ref[0, :, off:off + FF_CHUNK])
        return c[HALO:HALO + tm]

    for c in range(D_FF // FF_CHUNK):
        gate = conv_up(c * FF_CHUNK)
        val = conv_up(D_FF + c * FF_CHUNK)
        act = gate * (1.0 / (1.0 + jnp.exp(-gate))) * val
        act_ref[:, c * FF_CHUNK:(c + 1) * FF_CHUNK] = act.astype(BF16)
    y = x_ref[...] + jnp.dot(act_ref[...], wdn_ref[0], preferred_element_type=F32)
    if final_norm:
        y = _rms(y, gf_ref[...])
    out_ref[...] = y


def _ffn_call(x, gain, wup, cw, cb, wdn, gf, layer, final_norm, batch, seq):
    nt = seq // TM_FFN
    assert (TM_FFN + 2 * HALO) % (FFN_ROW_BLOCKS * HALO) == 0
    return pl.pallas_call(
        functools.partial(_ffn_kernel, nt=nt, final_norm=final_norm),
        grid=(batch * nt,),
        in_specs=_halo_in_specs(batch * nt, TM_FFN) + [
            _const_spec((1, D_MODEL)),
            _layer_spec((D_MODEL, 2 * D_FF), layer),
            _layer_spec((3, 2 * D_FF), layer),
            _layer_spec((1, 2 * D_FF), layer),
            _layer_spec((D_FF, D_MODEL), layer),
            _const_spec((1, D_MODEL)),
        ],
        out_specs=pl.BlockSpec((TM_FFN, D_MODEL), lambda i: (i, 0)),
        out_shape=jax.ShapeDtypeStruct(x.shape, F32),
        scratch_shapes=[
            pltpu.VMEM((TM_FFN + 2 * HALO, D_MODEL), BF16),
            pltpu.VMEM((TM_FFN, D_FF), BF16),
        ],
        compiler_params=_params("parallel"),
        name="conv_ffn",
    )(x, x, x, gain, wup, cw, cb, wdn, gf)


def _rope_tables(seq, gain, scale):
    t = jnp.arange(seq, dtype=jnp.int32)
    row = (t // GRID_W).astype(F32)
    col = (t % GRID_W).astype(F32)
    inv_freq = ROPE_THETA ** (-jnp.arange(ROPE_PAIRS, dtype=F32) / ROPE_PAIRS)
    ang_row = inv_freq[:, None] * row[None, :]
    ang_col = inv_freq[:, None] * col[None, :]
    ang = jnp.concatenate([ang_row, ang_row, ang_col, ang_col], axis=0)
    sign = jnp.concatenate([-jnp.ones((ROPE_PAIRS,), F32), jnp.ones((ROPE_PAIRS,), F32)] * 2)
    g = gain.astype(F32) * scale
    g_partner = _rope_rows(g[:, None])[:, 0]
    return jnp.cos(ang) * g[:, None], jnp.sin(ang) * (sign * g_partner)[:, None]


def kernel(x, mem, attn_norm, attn_w_qkv, attn_q_gain, attn_k_gain, attn_w_o, pool_norm, pool_w, pool_scale, xattn_norm, mem_norm, xattn_w_q, xattn_w_kv, xattn_w_o, ffn_norm, ffn_w_up, ffn_conv_w, ffn_conv_b, ffn_w_down, final_norm):
    batch, seq, d = x.shape
    depth = xattn_norm.shape[0]
    assert d == D_MODEL and seq % TM == 0 and seq % TM_FFN == 0 and seq % TQ == 0 and seq % TK == 0
    xf = x.reshape(batch * seq, d)

    kT_mem, v_mem = _memkv_call(mem, mem_norm.reshape(depth, 1, d),
                                xattn_w_kv.astype(BF16), depth, batch)

    xw_q, xw_o = xattn_w_q.astype(BF16), xattn_w_o.astype(BF16)
    f_up, f_down = ffn_w_up.astype(BF16), ffn_w_down.astype(BF16)
    f_cb = ffn_conv_b.reshape(depth, 1, 2 * D_FF)

    ia = ib = 0
    for i in range(depth):
        xa = (xattn_norm[i].reshape(1, d), xw_q, kT_mem, v_mem, xw_o)
        if i % 2 == 0:
            bound = (HEAD_DIM ** 0.5) * LOG2E * jnp.max(jnp.abs(attn_q_gain[ia])) * jnp.max(
                jnp.abs(attn_k_gain[ia]))
            fixed_ok = bound <= MAX_STATIC_SHIFT
            shift = jnp.where(fixed_ok, bound, 0.0)
            qx = jnp.zeros((QK_ROWS - HEAD_DIM, TM), F32).at[0].set(-shift).astype(BF16)
            kx = jnp.zeros((TM, QK_ROWS - HEAD_DIM), BF16).at[:, 0].set(1.0)
            cq, sq = _rope_tables(seq, attn_q_gain[ia], (HEAD_DIM ** -0.5) * LOG2E)
            ck, sk = _rope_tables(seq, attn_k_gain[ia], 1.0)
            qT, k, vT = _qkv_call(
                xf, attn_norm[ia].reshape(1, d), attn_w_qkv[ia].T.astype(BF16),
                cq, sq, ck, sk, qx, kx, batch, seq)
            oT = lax.cond(
                fixed_ok,
                functools.partial(_attn_call, batch=batch, seq=seq, online=False),
                functools.partial(_attn_call, batch=batch, seq=seq, online=True),
                qT, k, vT)
            xf = _attn_out_xattn_call(xf, oT, attn_w_o[ia].astype(BF16), *xa, i, batch, seq)
            ia += 1
        else:
            xf = _pool_xattn_call(xf, pool_norm[ib].reshape(1, d), pool_w[ib].astype(BF16),
                                  pool_scale[ib].reshape(1, d), *xa, i, batch, seq)
            ib += 1
        xf = _ffn_call(xf, ffn_norm[i].reshape(1, d), f_up, ffn_conv_w, f_cb, f_down,
                       final_norm.reshape(1, d), i, i == depth - 1, batch, seq)
    return xf.reshape(batch, seq, d)
```

```python
import functools
import math

import jax
import jax.numpy as jnp
from jax import lax
from jax.experimental import pallas as pl
from jax.experimental.pallas import tpu as pltpu

F32 = jnp.float32
BF16 = jnp.bfloat16

D_MODEL = 1024
GRID_W = 64
EPS = 1e-6
HEAD_DIM = 64
N_HEADS = 16
N_KV_HEADS = 4
GQA_GROUP = 4
ROPE_THETA = 10000.0
ROPE_PAIRS = 16
Q_DIM = N_HEADS * HEAD_DIM
KV_DIM = N_KV_HEADS * HEAD_DIM
QK_DIM = Q_DIM + KV_DIM
QKV_DIM = Q_DIM + 2 * KV_DIM
POOL_WINDOWS = (2, 4, 8, 16)
POOL_GROUP_W = 256
MEM_LEN = 256
X_HEADS = 4
X_HEAD_DIM = 256
D_FF = 2816
LOG2E = math.log2(math.e)

HALO = 16
SUM_ROWS = 16
VMEM_LIMIT = 56 * 1024 * 1024

TM = 1024
TM_FFN = 1024
TQ = 1024
TK = 256
QK_ROWS = 2 * HEAD_DIM
MAX_STATIC_SHIFT = 40.0
FF_CHUNK = 256
POOL_ROWS = 128
MEMKV_BATCHES = 4
QKV_ROW_BLOCKS = 6
FFN_ROW_BLOCKS = 3


def _rms(x, gain):
    return x * lax.rsqrt(jnp.mean(x * x, axis=-1, keepdims=True) + EPS) * gain


def _params(*sem):
    return pltpu.CompilerParams(dimension_semantics=sem, vmem_limit_bytes=VMEM_LIMIT)


def _dot_rows(a, b, blocks):
    m = a.shape[0] // blocks
    return jnp.concatenate(
        [jnp.dot(a[r * m:(r + 1) * m], b, preferred_element_type=F32) for r in range(blocks)],
        axis=0)


def _const_spec(shape):
    nd = len(shape)
    return pl.BlockSpec(shape, lambda *_: (0,) * nd)


def _rope_rows(y):
    p = ROPE_PAIRS
    return jnp.concatenate([y[p:2 * p], y[:p], y[3 * p:], y[2 * p:3 * p]], axis=0)


def _qkv_kernel(x_ref, g_ref, wT_ref, cq_ref, sq_ref, ck_ref, sk_ref, qx_ref, kx_ref,
                qT_ref, k_ref, vT_ref):
    h = _rms(x_ref[...], g_ref[...])
    yT = _dot_rows(wT_ref[...], h.T.astype(BF16), QKV_ROW_BLOCKS)

    def norm_rope(hd, cos, sin):
        y = yT[hd * HEAD_DIM:(hd + 1) * HEAD_DIM]
        r = lax.rsqrt(jnp.mean(y * y, axis=0, keepdims=True) + EPS)
        return (y * cos + _rope_rows(y) * sin) * r

    cq, sq = cq_ref[...], sq_ref[...]
    for hd in range(N_HEADS):
        qT_ref[0, hd, :HEAD_DIM, :] = norm_rope(hd, cq, sq).astype(BF16)
        qT_ref[0, hd, HEAD_DIM:, :] = qx_ref[...]
    ck, sk = ck_ref[...], sk_ref[...]
    kT = jnp.concatenate([norm_rope(N_HEADS + g, ck, sk) for g in range(N_KV_HEADS)], axis=0)
    k = kT.T.astype(BF16)
    ones = jnp.ones((SUM_ROWS, yT.shape[1]), BF16)
    for g in range(N_KV_HEADS):
        sl = slice(g * HEAD_DIM, (g + 1) * HEAD_DIM)
        k_ref[0, g] = jnp.concatenate([k[:, sl], kx_ref[...]], axis=1)
        vT = yT[QK_DIM + g * HEAD_DIM:QK_DIM + (g + 1) * HEAD_DIM].astype(BF16)
        vT_ref[0, g] = jnp.concatenate([vT, ones], axis=0)


def _qkv_call(x, gain, wT, cq, sq, ck, sk, qx, kx, batch, seq):
    nt = seq // TM
    v_rows = HEAD_DIM + SUM_ROWS
    table = pl.BlockSpec((HEAD_DIM, TM), lambda i: (0, i % nt))
    return pl.pallas_call(
        _qkv_kernel,
        grid=(batch * nt,),
        in_specs=[
            pl.BlockSpec((TM, D_MODEL), lambda i: (i, 0)),
            _const_spec((1, D_MODEL)),
            _const_spec((QKV_DIM, D_MODEL)),
            table, table, table, table,
            _const_spec((QK_ROWS - HEAD_DIM, TM)),
            _const_spec((TM, QK_ROWS - HEAD_DIM)),
        ],
        out_specs=[
            pl.BlockSpec((1, N_HEADS, QK_ROWS, TM), lambda i: (i // nt, 0, 0, i % nt)),
            pl.BlockSpec((1, N_KV_HEADS, TM, QK_ROWS), lambda i: (i // nt, 0, i % nt, 0)),
            pl.BlockSpec((1, N_KV_HEADS, v_rows, TM), lambda i: (i // nt, 0, 0, i % nt)),
        ],
        out_shape=[
            jax.ShapeDtypeStruct((batch, N_HEADS, QK_ROWS, seq), BF16),
            jax.ShapeDtypeStruct((batch, N_KV_HEADS, seq, QK_ROWS), BF16),
            jax.ShapeDtypeStruct((batch, N_KV_HEADS, v_rows, seq), BF16),
        ],
        compiler_params=_params("parallel"),
        name="qkv_rope",
    )(x, gain, wT, cq, sq, ck, sk, qx, kx)


def _attn_fixed_kernel(qT_ref, k_ref, vT_ref, oT_ref, p_ref):
    q = jnp.concatenate([qT_ref[0, g] for g in range(GQA_GROUP)], axis=1)
    s = jnp.dot(k_ref[0, 0], q, preferred_element_type=F32)
    seq = k_ref.shape[2]
    z = pl.multiple_of(jnp.minimum(pl.program_id(2), 0) * SUM_ROWS, SUM_ROWS)
    p_ref[pl.ds(z, seq), :] = jnp.exp2(s).astype(BF16)
    o = jnp.dot(vT_ref[0, 0], p_ref[pl.ds(z, seq), :], preferred_element_type=F32)
    o = o[:HEAD_DIM] / o[HEAD_DIM:HEAD_DIM + 1]
    for g in range(GQA_GROUP):
        oT_ref[0, g * HEAD_DIM:(g + 1) * HEAD_DIM, :] = o[:, g * TQ:(g + 1) * TQ].astype(BF16)


def _attn_online_kernel(qT_ref, k_ref, vT_ref, oT_ref, s0_ref, s1_ref, m_ref, acc_ref, *, seq):
    n_chunks = seq // TK

    def scores(c, s_ref):
        q = jnp.concatenate([qT_ref[0, g] for g in range(GQA_GROUP)], axis=1)
        kc = k_ref[0, 0, pl.ds(pl.multiple_of(c * TK, TK), TK), :]
        s_ref[...] = jnp.dot(kc, q, preferred_element_type=F32)

    def softmax_pv(c, s_ref):
        s = s_ref[...]
        m_prev = m_ref[...]
        m_new = jnp.maximum(m_prev, jnp.max(s, axis=0, keepdims=True))
        alpha = jnp.exp2(m_prev - m_new)
        p = jnp.exp2(s - m_new).astype(BF16)
        vc = vT_ref[0, 0, :, pl.ds(pl.multiple_of(c * TK, TK), TK)]
        acc_ref[...] = alpha * acc_ref[...] + jnp.dot(vc, p, preferred_element_type=F32)
        m_ref[...] = m_new

    m_ref[...] = jnp.full(m_ref.shape, -1e30, F32)
    acc_ref[...] = jnp.zeros(acc_ref.shape, F32)
    scores(0, s0_ref)

    def body(i, carry):
        c = 2 * i
        scores(c + 1, s1_ref)
        softmax_pv(c, s0_ref)
        scores(c + 2, s0_ref)
        softmax_pv(c + 1, s1_ref)
        return carry

    lax.fori_loop(0, n_chunks // 2 - 1, body, 0)
    scores(n_chunks - 1, s1_ref)
    softmax_pv(n_chunks - 2, s0_ref)
    softmax_pv(n_chunks - 1, s1_ref)
    acc = acc_ref[...]
    o = acc[:HEAD_DIM] / acc[HEAD_DIM:HEAD_DIM + 1]
    for g in range(GQA_GROUP):
        oT_ref[0, g * HEAD_DIM:(g + 1) * HEAD_DIM, :] = o[:, g * TQ:(g + 1) * TQ].astype(BF16)


def _attn_call(qT, k, vT, batch, seq, online):
    grp = GQA_GROUP * HEAD_DIM
    lanes = GQA_GROUP * TQ
    v_rows = HEAD_DIM + SUM_ROWS
    if online:
        body = functools.partial(_attn_online_kernel, seq=seq)
        scratch = [
            pltpu.VMEM((TK, lanes), F32),
            pltpu.VMEM((TK, lanes), F32),
            pltpu.VMEM((1, lanes), F32),
            pltpu.VMEM((v_rows, lanes), F32),
        ]
    else:
        body, scratch = _attn_fixed_kernel, [pltpu.VMEM((seq + SUM_ROWS, lanes), BF16)]
    return pl.pallas_call(
        body,
        grid=(batch, N_KV_HEADS, seq // TQ),
        in_specs=[
            pl.BlockSpec((1, GQA_GROUP, QK_ROWS, TQ), lambda b, h, i: (b, h, 0, i)),
            pl.BlockSpec((1, 1, seq, QK_ROWS), lambda b, h, i: (b, h, 0, 0)),
            pl.BlockSpec((1, 1, v_rows, seq), lambda b, h, i: (b, h, 0, 0)),
        ],
        out_specs=pl.BlockSpec((1, grp, TQ), lambda b, h, i: (b, h, i)),
        out_shape=jax.ShapeDtypeStruct((batch, Q_DIM, seq), BF16),
        scratch_shapes=scratch,
        compiler_params=_params("parallel", "parallel", "parallel"),
        name="axial_attention_online" if online else "axial_attention",
    )(qT, k, vT)


def _memkv_kernel(mem_ref, g_ref, w_ref, kT_ref, v_ref):
    nb = mem_ref.shape[0]
    h = _rms(mem_ref[...].reshape(nb * MEM_LEN, D_MODEL), g_ref[0])
    kv = jnp.dot(h.astype(BF16), w_ref[0], preferred_element_type=F32)
    for b in range(nb):
        rows = slice(b * MEM_LEN, (b + 1) * MEM_LEN)
        kT_ref[0, b] = kv[rows, :D_MODEL].T.astype(BF16)
        v_ref[0, b] = kv[rows, D_MODEL:].astype(BF16)


def _memkv_call(mem, gain, w, depth, batch):
    nb = MEMKV_BATCHES
    assert batch % nb == 0
    return pl.pallas_call(
        _memkv_kernel,
        grid=(depth, batch // nb),
        in_specs=[
            pl.BlockSpec((nb, MEM_LEN, D_MODEL), lambda l, b: (b, 0, 0)),
            pl.BlockSpec((1, 1, D_MODEL), lambda l, b: (l, 0, 0)),
            pl.BlockSpec((1, D_MODEL, 2 * D_MODEL), lambda l, b: (l, 0, 0)),
        ],
        out_specs=[
            pl.BlockSpec((1, nb, D_MODEL, MEM_LEN), lambda l, b: (l, b, 0, 0)),
            pl.BlockSpec((1, nb, MEM_LEN, D_MODEL), lambda l, b: (l, b, 0, 0)),
        ],
        out_shape=[
            jax.ShapeDtypeStruct((depth, batch, D_MODEL, MEM_LEN), BF16),
            jax.ShapeDtypeStruct((depth, batch, MEM_LEN, D_MODEL), BF16),
        ],
        compiler_params=_params("parallel", "parallel"),
        name="memory_kv",
    )(mem, gain, w)


def _xattn(x1, gx_ref, wq_ref, kT_ref, v_ref, wo_ref):
    h = _rms(x1, gx_ref[...])
    q = jnp.dot(h.astype(BF16), wq_ref[0], preferred_element_type=F32)
    q = q * ((X_HEAD_DIM ** -0.5) * LOG2E)
    outs = []
    for hd in range(X_HEADS):
        sl = slice(hd * X_HEAD_DIM, (hd + 1) * X_HEAD_DIM)
        s = jnp.dot(q[:, sl].astype(BF16), kT_ref[0, 0, sl, :],
                    preferred_element_type=F32)
        p = jnp.exp2(s - jnp.max(s, axis=-1, keepdims=True))
        l = jnp.sum(p, axis=-1, keepdims=True)
        o = jnp.dot(p.astype(BF16), v_ref[0, 0, :, sl], preferred_element_type=F32)
        outs.append((o / l).astype(BF16))
    o = jnp.concatenate(outs, axis=1)
    return x1 + jnp.dot(o, wo_ref[0], preferred_element_type=F32)


def _attn_out_xattn_kernel(x_ref, oT_ref, wo_ref, gx_ref, wq_ref, kT_ref, v_ref, wo2_ref,
                           out_ref):
    o = oT_ref[0].T
    x1 = x_ref[...] + jnp.dot(o, wo_ref[...], preferred_element_type=F32)
    out_ref[...] = _xattn(x1, gx_ref, wq_ref, kT_ref, v_ref, wo2_ref)


def _layer_spec(shape, layer):
    nd = len(shape)
    return pl.BlockSpec((1,) + tuple(shape), lambda *_: (layer,) + (0,) * nd)


def _xattn_specs(layer, nt):
    return [
        _const_spec((1, D_MODEL)),
        _layer_spec((D_MODEL, D_MODEL), layer),
        pl.BlockSpec((1, 1, D_MODEL, MEM_LEN), lambda i: (layer, i // nt, 0, 0)),
        pl.BlockSpec((1, 1, MEM_LEN, D_MODEL), lambda i: (layer, i // nt, 0, 0)),
        _layer_spec((D_MODEL, D_MODEL), layer),
    ]


def _attn_out_xattn_call(x, oT, wo, gx, wq, kT, v, wo2, layer, batch, seq):
    nt = seq // TM
    return pl.pallas_call(
        _attn_out_xattn_kernel,
        grid=(batch * nt,),
        in_specs=[
            pl.BlockSpec((TM, D_MODEL), lambda i: (i, 0)),
            pl.BlockSpec((1, Q_DIM, TM), lambda i: (i // nt, 0, i % nt)),
            _const_spec((Q_DIM, D_MODEL)),
        ] + _xattn_specs(layer, nt),
        out_specs=pl.BlockSpec((TM, D_MODEL), lambda i: (i, 0)),
        out_shape=jax.ShapeDtypeStruct(x.shape, F32),
        compiler_params=_params("parallel"),
        name="attn_out_xattn",
    )(x, oT, wo, gx, wq, kT, v, wo2)


def _normed_with_halo(xp_ref, x_ref, xn_ref, gain, nt):
    i = pl.program_id(0)
    keep_prev = (i % nt != 0).astype(F32)
    keep_next = (i % nt != nt - 1).astype(F32)
    hp = _rms(xp_ref[...], gain) * keep_prev
    hn = _rms(xn_ref[...], gain) * keep_next
    return jnp.concatenate([hp, _rms(x_ref[...], gain), hn], axis=0)


def _halo_in_specs(n_tiles, tm):
    blocks = tm // HALO
    n_halo_blocks = n_tiles * blocks
    return [
        pl.BlockSpec((HALO, D_MODEL), lambda i: (jnp.maximum(i * blocks - 1, 0), 0)),
        pl.BlockSpec((tm, D_MODEL), lambda i: (i, 0)),
        pl.BlockSpec((HALO, D_MODEL),
                     lambda i: (jnp.minimum((i + 1) * blocks, n_halo_blocks - 1), 0)),
    ]


def _pool_xattn_kernel(xp_ref, x_ref, xn_ref, gp_ref, band_ref, wp_ref, sc_ref,
                       gx_ref, wq_ref, kT_ref, v_ref, wo2_ref, out_ref, *, nt, seq):
    i = pl.program_id(0)
    h = _normed_with_halo(xp_ref, x_ref, xn_ref, gp_ref[...], nt)
    hb = h.astype(BF16)
    pos = (i % nt) * TM + lax.broadcasted_iota(jnp.int32, (TM, 1), 0)
    ys = []
    for g, w in enumerate(POOL_WINDOWS):
        sl = slice(g * POOL_GROUP_W, (g + 1) * POOL_GROUP_W)
        cnt = (jnp.minimum(pos + w // 2, seq) - jnp.maximum(pos - w // 2, 0)).astype(F32)
        sums = jnp.concatenate(
            [jnp.dot(band_ref[g], hb[r:r + POOL_ROWS + 2 * HALO, sl], preferred_element_type=F32)
             for r in range(0, TM, POOL_ROWS)], axis=0)
        mixed = sums * (1.0 / cnt) - h[HALO:HALO + TM, sl]
        ys.append(jnp.dot(mixed.astype(BF16), wp_ref[g], preferred_element_type=F32))
    y = jnp.concatenate(ys, axis=1) * sc_ref[...]
    x1 = x_ref[...] + y
    out_ref[...] = _xattn(x1, gx_ref, wq_ref, kT_ref, v_ref, wo2_ref)


def _pool_bands():
    i = jnp.arange(POOL_ROWS)[:, None]
    m = jnp.arange(POOL_ROWS + 2 * HALO)[None, :] - HALO
    return jnp.stack([((m >= i - w // 2) & (m < i + w // 2)).astype(BF16) for w in POOL_WINDOWS])


def _pool_xattn_call(x, gp, wp, sc, gx, wq, kT, v, wo2, layer, batch, seq):
    nt = seq // TM
    assert max(POOL_WINDOWS) // 2 <= HALO and TM % POOL_ROWS == 0
    return pl.pallas_call(
        functools.partial(_pool_xattn_kernel, nt=nt, seq=seq),
        grid=(batch * nt,),
        in_specs=_halo_in_specs(batch * nt, TM) + [
            _const_spec((1, D_MODEL)),
            _const_spec((len(POOL_WINDOWS), POOL_ROWS, POOL_ROWS + 2 * HALO)),
            _const_spec((len(POOL_WINDOWS), POOL_GROUP_W, POOL_GROUP_W)),
            _const_spec((1, D_MODEL)),
        ] + _xattn_specs(layer, nt),
        out_specs=pl.BlockSpec((TM, D_MODEL), lambda i: (i, 0)),
        out_shape=jax.ShapeDtypeStruct(x.shape, F32),
        compiler_params=_params("parallel"),
        name="pool_xattn",
    )(x, x, x, gp, _pool_bands(), wp, sc, gx, wq, kT, v, wo2)


def _ffn_kernel(xp_ref, x_ref, xn_ref, g_ref, wup_ref, cw_ref, cb_ref, wdn_ref, gf_ref,
                out_ref, h_ref, act_ref, *, nt, final_norm):
    tm = x_ref.shape[0]
    rows = tm + 2 * HALO
    h_ref[...] = _normed_with_halo(xp_ref, x_ref, xn_ref, g_ref[...], nt).astype(BF16)

    def conv_up(off):
        u = _dot_rows(h_ref[...], wup_ref[0, :, off:off + FF_CHUNK], FFN_ROW_BLOCKS)
        cw = cw_ref[0, :, off:off + FF_CHUNK]
        c = (pltpu.roll(u, 1, 0) * cw[0:1] + u * cw[1:2]
             + pltpu.roll(u, rows - 1, 0) * cw[2:3] + cb_ref[0, :, off:off + FF_CHUNK])
        return c[HALO:HALO + tm]

    for c in range(D_FF // FF_CHUNK):
        gate = conv_up(c * FF_CHUNK)
        val = conv_up(D_FF + c * FF_CHUNK)
        act = gate * (1.0 / (1.0 + jnp.exp(-gate))) * val
        act_ref[:, c * FF_CHUNK:(c + 1) * FF_CHUNK] = act.astype(BF16)
    y = x_ref[...] + jnp.dot(act_ref[...], wdn_ref[0], preferred_element_type=F32)
    if final_norm:
        y = _rms(y, gf_ref[...])
    out_ref[...] = y


def _ffn_call(x, gain, wup, cw, cb, wdn, gf, layer, final_norm, batch, seq):
    nt = seq // TM_FFN
    assert (TM_FFN + 2 * HALO) % (FFN_ROW_BLOCKS * HALO) == 0
    return pl.pallas_call(
        functools.partial(_ffn_kernel, nt=nt, final_norm=final_norm),
        grid=(batch * nt,),
        in_specs=_halo_in_specs(batch * nt, TM_FFN) + [
            _const_spec((1, D_MODEL)),
            _layer_spec((D_MODEL, 2 * D_FF), layer),
            _layer_spec((3, 2 * D_FF), layer),
            _layer_spec((1, 2 * D_FF), layer),
            _layer_spec((D_FF, D_MODEL), layer),
            _const_spec((1, D_MODEL)),
        ],
        out_specs=pl.BlockSpec((TM_FFN, D_MODEL), lambda i: (i, 0)),
        out_shape=jax.ShapeDtypeStruct(x.shape, F32),
        scratch_shapes=[
            pltpu.VMEM((TM_FFN + 2 * HALO, D_MODEL), BF16),
            pltpu.VMEM((TM_FFN, D_FF), BF16),
        ],
        compiler_params=_params("parallel"),
        name="conv_ffn",
    )(x, x, x, gain, wup, cw, cb, wdn, gf)


def _rope_tables(seq, gain, scale):
    t = jnp.arange(seq, dtype=jnp.int32)
    row = (t // GRID_W).astype(F32)
    col = (t % GRID_W).astype(F32)
    inv_freq = ROPE_THETA ** (-jnp.arange(ROPE_PAIRS, dtype=F32) / ROPE_PAIRS)
    ang_row = inv_freq[:, None] * row[None, :]
    ang_col = inv_freq[:, None] * col[None, :]
    ang = jnp.concatenate([ang_row, ang_row, ang_col, ang_col], axis=0)
    sign = jnp.concatenate([-jnp.ones((ROPE_PAIRS,), F32), jnp.ones((ROPE_PAIRS,), F32)] * 2)
    g = gain.astype(F32) * scale
    g_partner = _rope_rows(g[:, None])[:, 0]
    return jnp.cos(ang) * g[:, None], jnp.sin(ang) * (sign * g_partner)[:, None]


def kernel(x, mem, attn_norm, attn_w_qkv, attn_q_gain, attn_k_gain, attn_w_o, pool_norm, pool_w, pool_scale, xattn_norm, mem_norm, xattn_w_q, xattn_w_kv, xattn_w_o, ffn_norm, ffn_w_up, ffn_conv_w, ffn_conv_b, ffn_w_down, final_norm):
    batch, seq, d = x.shape
    depth = xattn_norm.shape[0]
    assert d == D_MODEL and seq % TM == 0 and seq % TM_FFN == 0 and seq % TQ == 0 and seq % TK == 0
    xf = x.reshape(batch * seq, d)

    kT_mem, v_mem = _memkv_call(mem, mem_norm.reshape(depth, 1, d),
                                xattn_w_kv.astype(BF16), depth, batch)

    xw_q, xw_o = xattn_w_q.astype(BF16), xattn_w_o.astype(BF16)
    f_up, f_down = ffn_w_up.astype(BF16), ffn_w_down.astype(BF16)
    f_cb = ffn_conv_b.reshape(depth, 1, 2 * D_FF)

    ia = ib = 0
    for i in range(depth):
        xa = (xattn_norm[i].reshape(1, d), xw_q, kT_mem, v_mem, xw_o)
        if i % 2 == 0:
            bound = (HEAD_DIM ** 0.5) * LOG2E * jnp.max(jnp.abs(attn_q_gain[ia])) * jnp.max(
                jnp.abs(attn_k_gain[ia]))
            fixed_ok = bound <= MAX_STATIC_SHIFT
            shift = jnp.where(fixed_ok, bound, 0.0)
            qx = jnp.zeros((QK_ROWS - HEAD_DIM, TM), F32).at[0].set(-shift).astype(BF16)
            kx = jnp.zeros((TM, QK_ROWS - HEAD_DIM), BF16).at[:, 0].set(1.0)
            cq, sq = _rope_tables(seq, attn_q_gain[ia], (HEAD_DIM ** -0.5) * LOG2E)
            ck, sk = _rope_tables(seq, attn_k_gain[ia], 1.0)
            qT, k, vT = _qkv_call(
                xf, attn_norm[ia].reshape(1, d), attn_w_qkv[ia].T.astype(BF16),
                cq, sq, ck, sk, qx, kx, batch, seq)
            oT = lax.cond(
                fixed_ok,
                functools.partial(_attn_call, batch=batch, seq=seq, online=False),
                functools.partial(_attn_call, batch=batch, seq=seq, online=True),
                qT, k, vT)
            xf = _attn_out_xattn_call(xf, oT, attn_w_o[ia].astype(BF16), *xa, i, batch, seq)
            ia += 1
        else:
            xf = _pool_xattn_call(xf, pool_norm[ib].reshape(1, d), pool_w[ib].astype(BF16),
                                  pool_scale[ib].reshape(1, d), *xa, i, batch, seq)
            ib += 1
        xf = _ffn_call(xf, ffn_norm[i].reshape(1, d), f_up, ffn_conv_w, f_cb, f_down,
                       final_norm.reshape(1, d), i, i == depth - 1, batch, seq)
    return xf.reshape(batch, seq, d)
```
